```python
import math
import jax, jax.numpy as jnp
from jax import lax
import numpy as np

D_MODEL = 1024
BATCH = 8
SEQ = 4096
DEPTH = 2

N_MIXERS = 2
N_CONV_LAYERS = (DEPTH + 1) // 2
N_ATTN_LAYERS = DEPTH // 2
D_FF = 2816
CONV_WIDTH = 31
N_HEADS = 16
HEAD_DIM = D_MODEL // N_HEADS
D_ATTN = N_HEADS * HEAD_DIM
BLOCK_Q = 128
RMS_EPS = 1e-6
LN_EPS = 1e-5

kernel_name = "macaron_conv_stickbreaking_hybrid"


def rms_norm(x, g):
    xf = x.astype(jnp.float32)
    y = xf * lax.rsqrt(jnp.mean(xf * xf, axis=-1, keepdims=True) + RMS_EPS)
    return (y * g.astype(jnp.float32)).astype(x.dtype)


def layer_norm(x, g, b):
    xf = x.astype(jnp.float32)
    mu = jnp.mean(xf, axis=-1, keepdims=True)
    var = jnp.mean(jnp.square(xf - mu), axis=-1, keepdims=True)
    y = (xf - mu) * lax.rsqrt(var + LN_EPS)
    return (y * g.astype(jnp.float32) + b.astype(jnp.float32)).astype(x.dtype)


def swiglu_ffn(h, w_in, w_out):
    gate, up = jnp.split(h @ w_in, 2, axis=-1)
    return (jax.nn.silu(gate) * up) @ w_out


def conformer_conv(h, w_pw1, b_pw1, w_dw, b_dw, ln_g, ln_b, w_pw2, b_pw2):
    val, gate = jnp.split(h @ w_pw1 + b_pw1, 2, axis=-1)
    u = val * jax.nn.sigmoid(gate)
    u = lax.conv_general_dilated(
        u, w_dw[:, None, :].astype(u.dtype),
        window_strides=(1,),
        padding=((CONV_WIDTH - 1, 0),),
        dimension_numbers=("NWC", "WIO", "NWC"),
        feature_group_count=D_MODEL,
    ) + b_dw
    u = jax.nn.silu(layer_norm(u, ln_g, ln_b))
    return u @ w_pw2 + b_pw2


def stick_breaking_attention(h, w_qkv, w_o):
    b, s, _ = h.shape
    qkv = (h @ w_qkv).reshape(b, s, 3, N_HEADS, HEAD_DIM)
    qkv = jnp.transpose(qkv, (2, 0, 3, 1, 4))
    q, k, v = qkv[0], qkv[1], qkv[2]
    n_blk = s // BLOCK_Q
    q_blocks = jnp.transpose(q.reshape(b, N_HEADS, n_blk, BLOCK_Q, HEAD_DIM), (2, 0, 1, 3, 4))
    starts = jnp.arange(n_blk, dtype=jnp.int32) * BLOCK_Q
    key_pos = jnp.arange(s, dtype=jnp.int32)
    scale = 1.0 / math.sqrt(HEAD_DIM)

    def one_block(args):
        qb, start = args
        z = jnp.einsum("bhqd,bhkd->bhqk", qb, k).astype(jnp.float32) * scale
        q_pos = start + jnp.arange(BLOCK_Q, dtype=jnp.int32)
        mask = key_pos[None, :] < q_pos[:, None]
        log_beta = jax.nn.log_sigmoid(z)
        log_1m = jnp.where(mask, log_beta - z, 0.0)
        suffix = lax.cumsum(log_1m, axis=3, reverse=True) - log_1m
        a = jnp.where(mask, jnp.exp(log_beta + suffix), 0.0)
        return jnp.einsum("bhqk,bhkd->bhqd", a.astype(v.dtype), v)

    out = lax.map(one_block, (q_blocks, starts))
    out = jnp.transpose(out, (1, 0, 3, 2, 4)).reshape(b, s, D_ATTN)
    return out @ w_o


def setup_inputs(seed: int = 0) -> dict:
    key = jax.random.key(seed)
    ks = jax.random.split(key, 16)
    f32 = jnp.float32
    nrm = lambda k, shape, scale: jax.random.normal(k, shape, f32) * scale
    return {
        "x": jax.random.normal(ks[0], (BATCH, SEQ, D_MODEL), f32),
        "norm_g": 1.0 + nrm(ks[1], (DEPTH, 3, D_MODEL), 0.01),
        "final_g": 1.0 + nrm(ks[2], (D_MODEL,), 0.01),
        "ffn_w_in": nrm(ks[3], (DEPTH, 2, D_MODEL, 2 * D_FF), D_MODEL ** -0.5),
        "ffn_w_out": nrm(ks[4], (DEPTH, 2, D_FF, D_MODEL), D_FF ** -0.5),
        "conv_w_pw1": nrm(ks[5], (N_CONV_LAYERS, D_MODEL, 2 * D_MODEL), D_MODEL ** -0.5),
        "conv_b_pw1": nrm(ks[6], (N_CONV_LAYERS, 2 * D_MODEL), 0.01),
        "conv_w_dw": nrm(ks[7], (N_CONV_LAYERS, CONV_WIDTH, D_MODEL), CONV_WIDTH ** -0.5),
        "conv_b_dw": nrm(ks[8], (N_CONV_LAYERS, D_MODEL), 0.01),
        "conv_ln_g": 1.0 + nrm(ks[9], (N_CONV_LAYERS, D_MODEL), 0.01),
        "conv_ln_b": nrm(ks[10], (N_CONV_LAYERS, D_MODEL), 0.01),
        "conv_w_pw2": nrm(ks[11], (N_CONV_LAYERS, D_MODEL, D_MODEL), D_MODEL ** -0.5),
        "conv_b_pw2": nrm(ks[12], (N_CONV_LAYERS, D_MODEL), 0.01),
        "attn_w_qkv": nrm(ks[13], (N_ATTN_LAYERS, D_MODEL, 3 * D_ATTN), D_MODEL ** -0.5),
        "attn_w_o": nrm(ks[14], (N_ATTN_LAYERS, D_ATTN, D_MODEL), D_ATTN ** -0.5),
    }


def reference(x, norm_g, final_g, ffn_w_in, ffn_w_out,
              conv_w_pw1, conv_b_pw1, conv_w_dw, conv_b_dw, conv_ln_g, conv_ln_b,
              conv_w_pw2, conv_b_pw2, attn_w_qkv, attn_w_o):
    for i in range(DEPTH):
        g = norm_g[i]
        x = x + 0.5 * swiglu_ffn(rms_norm(x, g[0]), ffn_w_in[i, 0], ffn_w_out[i, 0])
        h = rms_norm(x, g[1])
        j = i // N_MIXERS
        if i % N_MIXERS == 0:
            x = x + conformer_conv(h, conv_w_pw1[j], conv_b_pw1[j], conv_w_dw[j], conv_b_dw[j],
                                   conv_ln_g[j], conv_ln_b[j], conv_w_pw2[j], conv_b_pw2[j])
        else:
            x = x + stick_breaking_attention(h, attn_w_qkv[j], attn_w_o[j])
        x = x + 0.5 * swiglu_ffn(rms_norm(x, g[2]), ffn_w_in[i, 1], ffn_w_out[i, 1])
    return rms_norm(x, final_g)
```

```python
import functools

import jax
import jax.numpy as jnp
from jax import lax
from jax.experimental import pallas as pl
from jax.experimental.pallas import tpu as pltpu

N_HEADS = 16
CONV_WIDTH = 31
RMS_EPS = 1e-6
LN_EPS = 1e-5

V7X_LANES = 128
V7X_SUBLANES = 8
V7X_VMEM_BYTES = 64 * 1024 * 1024
VMEM_LIMIT_BYTES = 56 * 1024 * 1024

CONV_HALO = 32


def _params(semantics):
    return pltpu.CompilerParams(dimension_semantics=semantics,
                                vmem_limit_bytes=VMEM_LIMIT_BYTES)


def _resident(shape):
    zeros = (0,) * len(shape)
    return pl.BlockSpec(shape, lambda *_: zeros, pipeline_mode=pl.Buffered(1))


def _rms(x, g):
    return x * lax.rsqrt(jnp.mean(x * x, axis=-1, keepdims=True) + RMS_EPS) * g


def _dot(a, b):
    return jnp.dot(a, b, preferred_element_type=jnp.float32)


def _ffn_body(x_ref, g_ref, win_ref, wout_ref, *rest, d_ff, chunk, final_norm):
    if final_norm:
        fg_ref, o_ref, act_ref = rest
    else:
        o_ref, act_ref = rest
    x = x_ref[...]
    h = _rms(x, g_ref[...]).astype(jnp.bfloat16)
    for c in range(d_ff // chunk):
        lo = c * chunk
        gate = _dot(h, win_ref[:, lo:lo + chunk])
        up = _dot(h, win_ref[:, d_ff + lo:d_ff + lo + chunk])
        act_ref[:, lo:lo + chunk] = (gate * jax.nn.sigmoid(gate) * up).astype(jnp.bfloat16)
    y = x + 0.5 * _dot(act_ref[...], wout_ref[...])
    if final_norm:
        y = _rms(y, fg_ref[...])
    o_ref[...] = y


def _ffn(x, g, w_in, w_out, final_g=None, *, tm=512, chunk=256):
    t, d = x.shape
    d_ff = w_out.shape[0]
    assert t % tm == 0 and d_ff % chunk == 0
    final_norm = final_g is not None
    in_specs = [
        pl.BlockSpec((tm, d), lambda i: (i, 0)),
        _resident((1, d)),
        _resident((d, 2 * d_ff)),
        _resident((d_ff, d)),
    ]
    args = [x, g.reshape(1, d), w_in.astype(jnp.bfloat16), w_out.astype(jnp.bfloat16)]
    if final_norm:
        in_specs.append(_resident((1, d)))
        args.append(final_g.reshape(1, d))
    return pl.pallas_call(
        functools.partial(_ffn_body, d_ff=d_ff, chunk=chunk, final_norm=final_norm),
        out_shape=jax.ShapeDtypeStruct((t, d), jnp.float32),
        grid=(t // tm,),
        in_specs=in_specs,
        out_specs=pl.BlockSpec((tm, d), lambda i: (i, 0)),
        scratch_shapes=[pltpu.VMEM((tm, d_ff), jnp.bfloat16)],
        compiler_params=_params(("arbitrary",)),
        name="ffn",
    )(*args)


def _conv_body(x_ref, g_ref, w1_ref, b1_ref, wdw_ref, bdw_ref, lng_ref, lnb_ref,
               w2_ref, b2_ref, o_ref, u_ref, *, ts):
    d = x_ref.shape[-1]

    @pl.when(pl.program_id(1) == 0)
    def _():
        u_ref[0:CONV_HALO, :] = jnp.zeros((CONV_HALO, d), jnp.float32)

    x = x_ref[0]
    h = _rms(x, g_ref[...]).astype(jnp.bfloat16)
    val = _dot(h, w1_ref[:, 0:d]) + b1_ref[:, 0:d]
    gate = _dot(h, w1_ref[:, d:2 * d]) + b1_ref[:, d:2 * d]
    u_ref[CONV_HALO:CONV_HALO + ts, :] = val * jax.nn.sigmoid(gate)

    first = CONV_HALO - (CONV_WIDTH - 1)
    acc = jnp.broadcast_to(bdw_ref[...], (ts, d))
    for w in range(CONV_WIDTH):
        acc = acc + u_ref[first + w:first + w + ts, :] * wdw_ref[w:w + 1, :]
    u_ref[0:CONV_HALO, :] = u_ref[ts:ts + CONV_HALO, :]

    mu = jnp.mean(acc, axis=-1, keepdims=True)
    cen = acc - mu
    var = jnp.mean(cen * cen, axis=-1, keepdims=True)
    y = cen * lax.rsqrt(var + LN_EPS) * lng_ref[...] + lnb_ref[...]
    y = (y * jax.nn.sigmoid(y)).astype(jnp.bfloat16)
    o_ref[0] = x + _dot(y, w2_ref[...]) + b2_ref[...]


def _conv(x, g, w1, b1, wdw, bdw, lng, lnb, w2, b2, *, ts=512):
    b, s, d = x.shape
    assert s % ts == 0 and ts >= CONV_HALO and CONV_HALO >= CONV_WIDTH - 1
    row = lambda a: a.reshape(1, -1)
    return pl.pallas_call(
        functools.partial(_conv_body, ts=ts),
        out_shape=jax.ShapeDtypeStruct((b, s, d), jnp.float32),
        grid=(b, s // ts),
        in_specs=[
            pl.BlockSpec((1, ts, d), lambda bi, si: (bi, si, 0)),
            _resident((1, d)),
            _resident((d, 2 * d)),
            _resident((1, 2 * d)),
            _resident((CONV_WIDTH, d)),
            _resident((1, d)),
            _resident((1, d)),
            _resident((1, d)),
            _resident((d, d)),
            _resident((1, d)),
        ],
        out_specs=pl.BlockSpec((1, ts, d), lambda bi, si: (bi, si, 0)),
        scratch_shapes=[pltpu.VMEM((CONV_HALO + ts, d), jnp.float32)],
        compiler_params=_params(("arbitrary", "arbitrary")),
        name="conv",
    )(x, row(g), w1.astype(jnp.bfloat16), row(b1), wdw, row(bdw), row(lng), row(lnb),
      w2.astype(jnp.bfloat16), row(b2))


def _qkv_body(x_ref, g_ref, w_ref, o_ref, *, d_attn, scale):
    h = _rms(x_ref[...], g_ref[...]).astype(jnp.bfloat16)
    o_ref[:, 0:d_attn] = (_dot(h, w_ref[:, 0:d_attn]) * scale).astype(jnp.bfloat16)
    o_ref[:, d_attn:] = _dot(h, w_ref[:, d_attn:]).astype(jnp.bfloat16)


def _qkv(x, g, w, *, scale, tm=512):
    t, d = x.shape
    n = w.shape[1]
    assert t % tm == 0
    return pl.pallas_call(
        functools.partial(_qkv_body, d_attn=n // 3, scale=scale),
        out_shape=jax.ShapeDtypeStruct((t, n), jnp.bfloat16),
        grid=(t // tm,),
        in_specs=[pl.BlockSpec((tm, d), lambda i: (i, 0)), _resident((1, d)), _resident((d, n))],
        out_specs=pl.BlockSpec((tm, n), lambda i: (i, 0)),
        compiler_params=_params(("arbitrary",)),
        name="qkv",
    )(x, g.reshape(1, d), w.astype(jnp.bfloat16))


def _softplus(z):
    return jnp.maximum(z, 0.0) + jnp.log1p(jnp.exp(-jnp.abs(z)))


def _suffix_sum(sp, tri2):
    hi = sp.astype(jnp.bfloat16)
    lo = (sp - hi.astype(jnp.float32)).astype(jnp.bfloat16)
    return _dot(jnp.concatenate([hi, lo], axis=1), tri2)


def _attn_body(q_ref, k_ref, v_ref, o_ref, acc_ref, *, tq, hd):
    qi = pl.program_id(2)
    q2 = q_ref[0]
    lane = lax.broadcasted_iota(jnp.int32, (1, 2 * hd), 1)
    row = lax.broadcasted_iota(jnp.int32, (tq, tq), 0)
    col = lax.broadcasted_iota(jnp.int32, (tq, tq), 1)
    causal = col < row
    tri = causal.astype(jnp.bfloat16)
    tri2 = jnp.concatenate([tri, tri], axis=0)
    nt = (((1,), (1,)), ((), ()))

    acc_ref[...] = jnp.zeros_like(acc_ref)
    for head in range(2):
        in_head = (lane >= head * hd) & (lane < (head + 1) * hd)
        qh = jnp.where(in_head, q2, jnp.zeros_like(q2))

        def block(kj, c, masked):
            start = pl.multiple_of(kj * tq, tq)
            kb = k_ref[0, pl.ds(start, tq), :]
            vb = v_ref[0, pl.ds(start, tq), :]
            vb = jnp.where(in_head, vb, jnp.zeros_like(vb))
            z = lax.dot_general(qh, kb, nt, preferred_element_type=jnp.float32)
            sp = _softplus(z)
            if masked:
                sp = jnp.where(causal, sp, 0.0)
            suffix = _suffix_sum(sp, tri2)
            a = jnp.exp(z - sp - suffix - c)
            if masked:
                a = jnp.where(causal, a, 0.0)
            acc_ref[...] += _dot(a.astype(jnp.bfloat16), vb)
            return c + suffix[:, 0:1] + sp[:, 0:1]

        c = block(qi, jnp.zeros((tq, 1), jnp.float32), True)
        lax.fori_loop(0, qi, lambda it, c: block(qi - 1 - it, c, False), c)
    o_ref[0] = acc_ref[...].astype(o_ref.dtype)


def _attn(qkv, *, n_heads, tq=256):
    b, s, n = qkv.shape
    d_attn = n // 3
    hd = d_attn // n_heads
    assert 2 * hd == V7X_LANES and s % tq == 0
    pairs = n_heads // 2
    return pl.pallas_call(
        functools.partial(_attn_body, tq=tq, hd=hd),
        out_shape=jax.ShapeDtypeStruct((b, s, d_attn), jnp.bfloat16),
        grid=(b, pairs, s // tq),
        in_specs=[
            pl.BlockSpec((1, tq, 2 * hd), lambda bi, p, i: (bi, i, p)),
            pl.BlockSpec((1, s, 2 * hd), lambda bi, p, i: (bi, 0, pairs + p)),
            pl.BlockSpec((1, s, 2 * hd), lambda bi, p, i: (bi, 0, 2 * pairs + p)),
        ],
        out_specs=pl.BlockSpec((1, tq, 2 * hd), lambda bi, p, i: (bi, i, p)),
        scratch_shapes=[pltpu.VMEM((tq, 2 * hd), jnp.float32)],
        compiler_params=_params(("arbitrary", "arbitrary", "arbitrary")),
        name="attn",
    )(qkv, qkv, qkv)


def _proj_body(x_ref, a_ref, w_ref, o_ref):
    o_ref[...] = x_ref[...] + _dot(a_ref[...], w_ref[...])


def _proj(x, a, w, *, tm=512):
    t, d = x.shape
    k = a.shape[1]
    assert t % tm == 0
    return pl.pallas_call(
        _proj_body,
        out_shape=jax.ShapeDtypeStruct((t, d), jnp.float32),
        grid=(t // tm,),
        in_specs=[pl.BlockSpec((tm, d), lambda i: (i, 0)),
                  pl.BlockSpec((tm, k), lambda i: (i, 0)),
                  _resident((k, d))],
        out_specs=pl.BlockSpec((tm, d), lambda i: (i, 0)),
        compiler_params=_params(("arbitrary",)),
        name="proj",
    )(x, a, w.astype(jnp.bfloat16))


def kernel(x, norm_g, final_g, ffn_w_in, ffn_w_out, conv_w_pw1, conv_b_pw1, conv_w_dw, conv_b_dw,
           conv_ln_g, conv_ln_b, conv_w_pw2, conv_b_pw2, attn_w_qkv, attn_w_o):
    b, s, d = x.shape
    depth = norm_g.shape[0]
    x = x.reshape(b * s, d)
    for i in range(depth):
        g = norm_g[i]
        j = i // 2
        x = _ffn(x, g[0], ffn_w_in[i, 0], ffn_w_out[i, 0])
        if i % 2 == 0:
            x = _conv(x.reshape(b, s, d), g[1], conv_w_pw1[j], conv_b_pw1[j], conv_w_dw[j],
                      conv_b_dw[j], conv_ln_g[j], conv_ln_b[j], conv_w_pw2[j],
                      conv_b_pw2[j]).reshape(b * s, d)
        else:
            hd = attn_w_qkv.shape[2] // 3 // N_HEADS
            qkv = _qkv(x, g[1], attn_w_qkv[j], scale=hd ** -0.5)
            a = _attn(qkv.reshape(b, s, -1), n_heads=N_HEADS)
            x = _proj(x, a.reshape(b * s, -1), attn_w_o[j])
        last = i == depth - 1
        x = _ffn(x, g[2], ffn_w_in[i, 1], ffn_w_out[i, 1], final_g if last else None)
    return x.reshape(b, s, d)
```

```python
import functools

import jax
import jax.numpy as jnp
from jax import lax
from jax.experimental import pallas as pl
from jax.experimental.pallas import tpu as pltpu

N_HEADS = 16
CONV_WIDTH = 31
RMS_EPS = 1e-6
LN_EPS = 1e-5

V7X_LANES = 128
V7X_SUBLANES = 8
V7X_VMEM_BYTES = 64 * 1024 * 1024
VMEM_LIMIT_BYTES = 56 * 1024 * 1024

CONV_HALO = 32


def _params(semantics):
    return pltpu.CompilerParams(dimension_semantics=semantics,
                                vmem_limit_bytes=VMEM_LIMIT_BYTES)


def _resident(shape):
    zeros = (0,) * len(shape)
    return pl.BlockSpec(shape, lambda *_: zeros, pipeline_mode=pl.Buffered(1))


def _rms(x, g):
    return x * lax.rsqrt(jnp.mean(x * x, axis=-1, keepdims=True) + RMS_EPS) * g


def _dot(a, b):
    return jnp.dot(a, b, preferred_element_type=jnp.float32)


def _ffn_body(x_ref, g_ref, win_ref, wout_ref, *rest, d_ff, chunk, final_norm):
    if final_norm:
        fg_ref, o_ref, act_ref = rest
    else:
        o_ref, act_ref = rest
    x = x_ref[...]
    h = _rms(x, g_ref[...]).astype(jnp.bfloat16)
    for c in range(d_ff // chunk):
        lo = c * chunk
        gate = _dot(h, win_ref[:, lo:lo + chunk])
        up = _dot(h, win_ref[:, d_ff + lo:d_ff + lo + chunk])
        act_ref[:, lo:lo + chunk] = (gate * jax.nn.sigmoid(gate) * up).astype(jnp.bfloat16)
    y = x + 0.5 * _dot(act_ref[...], wout_ref[...])
    if final_norm:
        y = _rms(y, fg_ref[...])
    o_ref[...] = y


def _ffn(x, g, w_in, w_out, final_g=None, *, tm=512, chunk=256):
    t, d = x.shape
    d_ff = w_out.shape[0]
    assert t % tm == 0 and d_ff % chunk == 0
    final_norm = final_g is not None
    in_specs = [
        pl.BlockSpec((tm, d), lambda i: (i, 0)),
        _resident((1, d)),
        _resident((d, 2 * d_ff)),
        _resident((d_ff, d)),
    ]
    args = [x, g.reshape(1, d), w_in.astype(jnp.bfloat16), w_out.astype(jnp.bfloat16)]
    if final_norm:
        in_specs.append(_resident((1, d)))
        args.append(final_g.reshape(1, d))
    return pl.pallas_call(
        functools.partial(_ffn_body, d_ff=d_ff, chunk=chunk, final_norm=final_norm),
        out_shape=jax.ShapeDtypeStruct((t, d), jnp.float32),
        grid=(t // tm,),
        in_specs=in_specs,
        out_specs=pl.BlockSpec((tm, d), lambda i: (i, 0)),
        scratch_shapes=[pltpu.VMEM((tm, d_ff), jnp.bfloat16)],
        compiler_params=_params(("arbitrary",)),
        name="ffn",
    )(*args)


def _conv_body(x_ref, g_ref, w1_ref, b1_ref, wdw_ref, bdw_ref, lng_ref, lnb_ref,
               w2_ref, b2_ref, o_ref, u_ref, *, ts):
    d = x_ref.shape[-1]

    @pl.when(pl.program_id(1) == 0)
    def _():
        u_ref[0:CONV_HALO, :] = jnp.zeros((CONV_HALO, d), jnp.float32)

    x = x_ref[0]
    h = _rms(x, g_ref[...]).astype(jnp.bfloat16)
    val = _dot(h, w1_ref[:, 0:d]) + b1_ref[:, 0:d]
    gate = _dot(h, w1_ref[:, d:2 * d]) + b1_ref[:, d:2 * d]
    u_ref[CONV_HALO:CONV_HALO + ts, :] = val * jax.nn.sigmoid(gate)

    first = CONV_HALO - (CONV_WIDTH - 1)
    acc = jnp.broadcast_to(bdw_ref[...], (ts, d))
    for w in range(CONV_WIDTH):
        acc = acc + u_ref[first + w:first + w + ts, :] * wdw_ref[w:w + 1, :]
    u_ref[0:CONV_HALO, :] = u_ref[ts:ts + CONV_HALO, :]

    mu = jnp.mean(acc, axis=-1, keepdims=True)
    cen = acc - mu
    var = jnp.mean(cen * cen, axis=-1, keepdims=True)
    y = cen * lax.rsqrt(var + LN_EPS) * lng_ref[...] + lnb_ref[...]
    y = (y * jax.nn.sigmoid(y)).astype(jnp.bfloat16)
    o_ref[0] = x + _dot(y, w2_ref[...]) + b2_ref[...]


def _conv(x, g, w1, b1, wdw, bdw, lng, lnb, w2, b2, *, ts=512):
    b, s, d = x.shape
    assert s % ts == 0 and ts >= CONV_HALO and CONV_HALO >= CONV_WIDTH - 1
    row = lambda a: a.reshape(1, -1)
    return pl.pallas_call(
        functools.partial(_conv_body, ts=ts),
        out_shape=jax.ShapeDtypeStruct((b, s, d), jnp.float32),
        grid=(b, s // ts),
        in_specs=[
            pl.BlockSpec((1, ts, d), lambda bi, si: (bi, si, 0)),
            _resident((1, d)),
            _resident((d, 2 * d)),
            _resident((1, 2 * d)),
            _resident((CONV_WIDTH, d)),
            _resident((1, d)),
            _resident((1, d)),
            _resident((1, d)),
            _resident((d, d)),
            _resident((1, d)),
        ],
        out_specs=pl.BlockSpec((1, ts, d), lambda bi, si: (bi, si, 0)),
        scratch_shapes=[pltpu.VMEM((CONV_HALO + ts, d), jnp.float32)],
        compiler_params=_params(("arbitrary", "arbitrary")),
        name="conv",
    )(x, row(g), w1.astype(jnp.bfloat16), row(b1), wdw, row(bdw), row(lng), row(lnb),
      w2.astype(jnp.bfloat16), row(b2))


def _qkv_body(x_ref, g_ref, w_ref, o_ref, *, d_attn, scale):
    h = _rms(x_ref[...], g_ref[...]).astype(jnp.bfloat16)
    o_ref[:, 0:d_attn] = (_dot(h, w_ref[:, 0:d_attn]) * scale).astype(jnp.bfloat16)
    o_ref[:, d_attn:] = _dot(h, w_ref[:, d_attn:]).astype(jnp.bfloat16)


def _qkv(x, g, w, *, scale, tm=512):
    t, d = x.shape
    n = w.shape[1]
    assert t % tm == 0
    return pl.pallas_call(
        functools.partial(_qkv_body, d_attn=n // 3, scale=scale),
        out_shape=jax.ShapeDtypeStruct((t, n), jnp.bfloat16),
        grid=(t // tm,),
        in_specs=[pl.BlockSpec((tm, d), lambda i: (i, 0)), _resident((1, d)), _resident((d, n))],
        out_specs=pl.BlockSpec((tm, n), lambda i: (i, 0)),
        compiler_params=_params(("arbitrary",)),
        name="qkv",
    )(x, g.reshape(1, d), w.astype(jnp.bfloat16))


LOG2_E = 1.4426950408889634


def _softplus2(y):
    neg_abs = pltpu.bitcast(pltpu.bitcast(y, jnp.uint32) | jnp.uint32(0x80000000), jnp.float32)
    return jnp.maximum(y, 0.0) + jnp.log2(1.0 + jnp.exp2(neg_abs))


def _suffix_sum(sp, tri2):
    hi = sp.astype(jnp.bfloat16)
    lo = (sp - hi.astype(jnp.float32)).astype(jnp.bfloat16)
    return _dot(jnp.concatenate([hi, lo], axis=1), tri2)


def _attn_body(q_ref, k_ref, v_ref, o_ref, acc_ref, *, tq, tk, hd):
    qi = pl.program_id(2)
    nsub = tq // tk
    lane = lax.broadcasted_iota(jnp.int32, (1, 2 * hd), 1)
    head_lanes = (lane < hd, lane >= hd)
    row = lax.broadcasted_iota(jnp.int32, (tk, tk), 0)
    col = lax.broadcasted_iota(jnp.int32, (tk, tk), 1)
    causal = col < row
    tri = causal.astype(jnp.bfloat16)
    tri2 = jnp.concatenate([tri, tri], axis=0)
    nt = (((1,), (1,)), ((), ()))
    q2 = q_ref[0]
    qh = [jnp.where(m, q2, jnp.zeros_like(q2)) for m in head_lanes]

    def step(kb, lo, hi, cs, masked):
        start = pl.multiple_of(kb * tk, tk)
        k = k_ref[0, pl.ds(start, tk), :]
        v = v_ref[0, pl.ds(start, tk), :]
        v2 = jnp.concatenate([jnp.where(m, v, jnp.zeros_like(v)) for m in head_lanes], axis=0)
        a_heads, new_cs = [], []
        for h in range(2):
            z = lax.dot_general(qh[h][lo:hi], k, nt, preferred_element_type=jnp.float32)
            z = z * LOG2_E
            sp = _softplus2(z)
            if masked:
                sp = jnp.where(causal, sp, 0.0)
            suffix = _suffix_sum(sp, tri2)
            a = jnp.exp2(z - sp - suffix - cs[h])
            if masked:
                a = jnp.where(causal, a, 0.0)
            a_heads.append(a.astype(jnp.bfloat16))
            new_cs.append(cs[h] + suffix[:, 0:1] + sp[:, 0:1])
        acc_ref[lo:hi, :] += _dot(jnp.concatenate(a_heads, axis=1), v2)
        return new_cs

    acc_ref[...] = jnp.zeros_like(acc_ref)
    base = qi * nsub
    zero = jnp.zeros((tk, 1), jnp.float32)
    cs = [[zero, zero] for _ in range(nsub)]
    for d in reversed(range(nsub)):
        cs[d] = step(base + d, d * tk, (d + 1) * tk, cs[d], True)
        for r in range(d + 1, nsub):
            cs[r] = step(base + d, r * tk, (r + 1) * tk, cs[r], False)
    carry = tuple(jnp.concatenate([cs[r][h] for r in range(nsub)], axis=0) for h in range(2))
    lax.fori_loop(0, base, lambda it, c: tuple(step(base - 1 - it, 0, tq, c, False)), carry)
    o_ref[0] = acc_ref[...].astype(o_ref.dtype)


def _attn(qkv, *, n_heads, tq=512, tk=256):
    b, s, n = qkv.shape
    d_attn = n // 3
    hd = d_attn // n_heads
    assert 2 * hd == V7X_LANES and s % tq == 0 and tq % tk == 0
    pairs = n_heads // 2
    return pl.pallas_call(
        functools.partial(_attn_body, tq=tq, tk=tk, hd=hd),
        out_shape=jax.ShapeDtypeStruct((b, s, d_attn), jnp.bfloat16),
        grid=(b, pairs, s // tq),
        in_specs=[
            pl.BlockSpec((1, tq, 2 * hd), lambda bi, p, i: (bi, i, p)),
            pl.BlockSpec((1, s, 2 * hd), lambda bi, p, i: (bi, 0, pairs + p)),
            pl.BlockSpec((1, s, 2 * hd), lambda bi, p, i: (bi, 0, 2 * pairs + p)),
        ],
        out_specs=pl.BlockSpec((1, tq, 2 * hd), lambda bi, p, i: (bi, i, p)),
        scratch_shapes=[pltpu.VMEM((tq, 2 * hd), jnp.float32)],
        compiler_params=_params(("arbitrary", "arbitrary", "arbitrary")),
        name="attn",
    )(qkv, qkv, qkv)


def _proj_body(x_ref, a_ref, w_ref, o_ref):
    o_ref[...] = x_ref[...] + _dot(a_ref[...], w_ref[...])


def _proj(x, a, w, *, tm=512):
    t, d = x.shape
    k = a.shape[1]
    assert t % tm == 0
    return pl.pallas_call(
        _proj_body,
        out_shape=jax.ShapeDtypeStruct((t, d), jnp.float32),
        grid=(t // tm,),
        in_specs=[pl.BlockSpec((tm, d), lambda i: (i, 0)),
                  pl.BlockSpec((tm, k), lambda i: (i, 0)),
                  _resident((k, d))],
        out_specs=pl.BlockSpec((tm, d), lambda i: (i, 0)),
        compiler_params=_params(("arbitrary",)),
        name="proj",
    )(x, a, w.astype(jnp.bfloat16))


def kernel(x, norm_g, final_g, ffn_w_in, ffn_w_out, conv_w_pw1, conv_b_pw1, conv_w_dw, conv_b_dw,
           conv_ln_g, conv_ln_b, conv_w_pw2, conv_b_pw2, attn_w_qkv, attn_w_o):
    b, s, d = x.shape
    depth = norm_g.shape[0]
    x = x.reshape(b * s, d)
    for i in range(depth):
        g = norm_g[i]
        j = i // 2
        x = _ffn(x, g[0], ffn_w_in[i, 0], ffn_w_out[i, 0])
        if i % 2 == 0:
            x = _conv(x.reshape(b, s, d), g[1], conv_w_pw1[j], conv_b_pw1[j], conv_w_dw[j],
                      conv_b_dw[j], conv_ln_g[j], conv_ln_b[j], conv_w_pw2[j],
                      conv_b_pw2[j]).reshape(b * s, d)
        else:
            hd = attn_w_qkv.shape[2] // 3 // N_HEADS
            qkv = _qkv(x, g[1], attn_w_qkv[j], scale=hd ** -0.5)
            a = _attn(qkv.reshape(b, s, -1), n_heads=N_HEADS)
            x = _proj(x, a.reshape(b * s, -1), attn_w_o[j])
        last = i == depth - 1
        x = _ffn(x, g[2], ffn_w_in[i, 1], ffn_w_out[i, 1], final_g if last else None)
    return x.reshape(b, s, d)
```

```python
import functools

import jax
import jax.numpy as jnp
from jax import lax
from jax.experimental import pallas as pl
from jax.experimental.pallas import tpu as pltpu

N_HEADS = 16
CONV_WIDTH = 31
RMS_EPS = 1e-6
LN_EPS = 1e-5

V7X_LANES = 128
V7X_SUBLANES = 8
V7X_VMEM_BYTES = 64 * 1024 * 1024
VMEM_LIMIT_BYTES = 56 * 1024 * 1024

CONV_HALO = 32


def _params(semantics):
    return pltpu.CompilerParams(dimension_semantics=semantics,
                                vmem_limit_bytes=VMEM_LIMIT_BYTES)


def _resident(shape):
    zeros = (0,) * len(shape)
    return pl.BlockSpec(shape, lambda *_: zeros, pipeline_mode=pl.Buffered(1))


def _rms(x, g):
    return x * lax.rsqrt(jnp.mean(x * x, axis=-1, keepdims=True) + RMS_EPS) * g


def _dot(a, b):
    return jnp.dot(a, b, preferred_element_type=jnp.float32)


def _ffn_body(x_ref, g_ref, win_ref, wout_ref, *rest, d_ff, chunk, final_norm):
    if final_norm:
        fg_ref, o_ref, act_ref = rest
    else:
        o_ref, act_ref = rest
    x = x_ref[...]
    h = _rms(x, g_ref[...]).astype(jnp.bfloat16)
    for c in range(d_ff // chunk):
        lo = c * chunk
        gate = _dot(h, win_ref[:, lo:lo + chunk])
        up = _dot(h, win_ref[:, d_ff + lo:d_ff + lo + chunk])
        act_ref[:, lo:lo + chunk] = (gate * jax.nn.sigmoid(gate) * up).astype(jnp.bfloat16)
    y = x + 0.5 * _dot(act_ref[...], wout_ref[...])
    if final_norm:
        y = _rms(y, fg_ref[...])
    o_ref[...] = y


def _ffn(x, g, w_in, w_out, final_g=None, *, tm=512, chunk=256):
    t, d = x.shape
    d_ff = w_out.shape[0]
    assert t % tm == 0 and d_ff % chunk == 0
    final_norm = final_g is not None
    in_specs = [
        pl.BlockSpec((tm, d), lambda i: (i, 0)),
        _resident((1, d)),
        _resident((d, 2 * d_ff)),
        _resident((d_ff, d)),
    ]
    args = [x, g.reshape(1, d), w_in.astype(jnp.bfloat16), w_out.astype(jnp.bfloat16)]
    if final_norm:
        in_specs.append(_resident((1, d)))
        args.append(final_g.reshape(1, d))
    return pl.pallas_call(
        functools.partial(_ffn_body, d_ff=d_ff, chunk=chunk, final_norm=final_norm),
        out_shape=jax.ShapeDtypeStruct((t, d), jnp.float32),
        grid=(t // tm,),
        in_specs=in_specs,
        out_specs=pl.BlockSpec((tm, d), lambda i: (i, 0)),
        scratch_shapes=[pltpu.VMEM((tm, d_ff), jnp.bfloat16)],
        compiler_params=_params(("arbitrary",)),
        name="ffn",
    )(*args)


CONV_ROWS = 128
CONV_LANES = 128


def _dwconv_chunk(u_ref, wdw_ref, bdw_ref, y_ref, t0, l0):
    lanes = pl.ds(l0, CONV_LANES)
    first = CONV_HALO - (CONV_WIDTH - 1)
    acc = jnp.broadcast_to(bdw_ref[:, lanes], (CONV_ROWS, CONV_LANES))
    for r in range(V7X_SUBLANES):
        part = None
        for w in range(CONV_WIDTH):
            off = first + w
            if off % V7X_SUBLANES != r:
                continue
            slab = u_ref[pl.ds(pl.multiple_of(t0 + (off - r), V7X_SUBLANES),
                               CONV_ROWS + (V7X_SUBLANES if r else 0)), lanes]
            term = slab * wdw_ref[pl.ds(w, 1), lanes]
            part = term if part is None else part + term
        acc = acc + part[r:r + CONV_ROWS]
    y_ref[pl.ds(t0, CONV_ROWS), lanes] = acc


def _conv_body(x_ref, g_ref, w1_ref, b1_ref, wdw_ref, bdw_ref, lng_ref, lnb_ref,
               w2_ref, b2_ref, o_ref, u_ref, y_ref, *, ts):
    d = x_ref.shape[-1]

    @pl.when(pl.program_id(1) == 0)
    def _():
        u_ref[0:CONV_HALO, :] = jnp.zeros((CONV_HALO, d), jnp.float32)

    x = x_ref[0]
    h = _rms(x, g_ref[...]).astype(jnp.bfloat16)
    val = _dot(h, w1_ref[:, 0:d]) + b1_ref[:, 0:d]
    gate = _dot(h, w1_ref[:, d:2 * d]) + b1_ref[:, d:2 * d]
    u_ref[CONV_HALO:CONV_HALO + ts, :] = val * jax.nn.sigmoid(gate)

    def chunk(i, carry):
        _dwconv_chunk(u_ref, wdw_ref, bdw_ref, y_ref,
                      pl.multiple_of((i // (d // CONV_LANES)) * CONV_ROWS, CONV_ROWS),
                      pl.multiple_of((i % (d // CONV_LANES)) * CONV_LANES, CONV_LANES))
        return carry

    lax.fori_loop(0, (ts // CONV_ROWS) * (d // CONV_LANES), chunk, 0)
    u_ref[0:CONV_HALO, :] = u_ref[ts:ts + CONV_HALO, :]

    acc = y_ref[...]
    mu = jnp.mean(acc, axis=-1, keepdims=True)
    cen = acc - mu
    var = jnp.mean(cen * cen, axis=-1, keepdims=True)
    y = cen * lax.rsqrt(var + LN_EPS) * lng_ref[...] + lnb_ref[...]
    y = (y * jax.nn.sigmoid(y)).astype(jnp.bfloat16)
    o_ref[0] = x + _dot(y, w2_ref[...]) + b2_ref[...]


def _conv(x, g, w1, b1, wdw, bdw, lng, lnb, w2, b2, *, ts=512):
    b, s, d = x.shape
    assert s % ts == 0 and ts % CONV_ROWS == 0 and d % CONV_LANES == 0
    assert ts >= CONV_HALO and CONV_WIDTH - 1 <= CONV_HALO and CONV_HALO % V7X_SUBLANES == 0
    row = lambda a: a.reshape(1, -1)
    return pl.pallas_call(
        functools.partial(_conv_body, ts=ts),
        out_shape=jax.ShapeDtypeStruct((b, s, d), jnp.float32),
        grid=(b, s // ts),
        in_specs=[
            pl.BlockSpec((1, ts, d), lambda bi, si: (bi, si, 0)),
            _resident((1, d)),
            _resident((d, 2 * d)),
            _resident((1, 2 * d)),
            _resident((CONV_WIDTH, d)),
            _resident((1, d)),
            _resident((1, d)),
            _resident((1, d)),
            _resident((d, d)),
            _resident((1, d)),
        ],
        out_specs=pl.BlockSpec((1, ts, d), lambda bi, si: (bi, si, 0)),
        scratch_shapes=[pltpu.VMEM((CONV_HALO + ts, d), jnp.float32),
                        pltpu.VMEM((ts, d), jnp.float32)],
        compiler_params=_params(("arbitrary", "arbitrary")),
        name="conv",
    )(x, row(g), w1.astype(jnp.bfloat16), row(b1), wdw, row(bdw), row(lng), row(lnb),
      w2.astype(jnp.bfloat16), row(b2))


def _qkv_body(x_ref, g_ref, w_ref, o_ref, *, d_attn, scale):
    h = _rms(x_ref[...], g_ref[...]).astype(jnp.bfloat16)
    o_ref[:, 0:d_attn] = (_dot(h, w_ref[:, 0:d_attn]) * scale).astype(jnp.bfloat16)
    o_ref[:, d_attn:] = _dot(h, w_ref[:, d_attn:]).astype(jnp.bfloat16)


def _qkv(x, g, w, *, scale, tm=512):
    t, d = x.shape
    n = w.shape[1]
    assert t % tm == 0
    return pl.pallas_call(
        functools.partial(_qkv_body, d_attn=n // 3, scale=scale),
        out_shape=jax.ShapeDtypeStruct((t, n), jnp.bfloat16),
        grid=(t // tm,),
        in_specs=[pl.BlockSpec((tm, d), lambda i: (i, 0)), _resident((1, d)), _resident((d, n))],
        out_specs=pl.BlockSpec((tm, n), lambda i: (i, 0)),
        compiler_params=_params(("arbitrary",)),
        name="qkv",
    )(x, g.reshape(1, d), w.astype(jnp.bfloat16))


LOG2_E = 1.4426950408889634


def _softplus2(y):
    neg_abs = pltpu.bitcast(pltpu.bitcast(y, jnp.uint32) | jnp.uint32(0x80000000), jnp.float32)
    return jnp.maximum(y, 0.0) + jnp.log2(1.0 + jnp.exp2(neg_abs))


def _suffix_sum(sp, tri2):
    hi = sp.astype(jnp.bfloat16)
    lo = (sp - hi.astype(jnp.float32)).astype(jnp.bfloat16)
    return _dot(jnp.concatenate([hi, lo], axis=1), tri2)


DEAD_LOG2 = 160.0


def _attn_body(q_ref, k_ref, v_ref, tri2_ref, o_ref, qh_ref, c_ref, acc_ref, *, tk, hd):
    qi = pl.program_id(2)
    base = 2 * qi
    lane = lax.broadcasted_iota(jnp.int32, (1, 2 * hd), 1)
    head_lanes = (lane < hd, lane >= hd)
    row = lax.broadcasted_iota(jnp.int32, (tk, tk), 0)
    col = lax.broadcasted_iota(jnp.int32, (tk, tk), 1)
    causal = col < row
    nt = (((1,), (1,)), ((), ()))

    q2 = q_ref[0]
    for h in range(2):
        qh_ref[h] = jnp.where(head_lanes[h], q2, jnp.zeros_like(q2))
    c_ref[...] = jnp.zeros_like(c_ref)
    acc_ref[...] = jnp.zeros_like(acc_ref)

    def step(kb, r, masked):
        rows = pl.ds(r * tk, tk)
        start = pl.multiple_of(kb * tk, tk)
        k = k_ref[0, pl.ds(start, tk), :]
        v = v_ref[0, pl.ds(start, tk), :]
        v2 = jnp.concatenate([jnp.where(m, v, jnp.zeros_like(v)) for m in head_lanes], axis=0)
        a_heads = []
        for h in range(2):
            z = lax.dot_general(qh_ref[h, rows, :], k, nt, preferred_element_type=jnp.float32)
            z = z * LOG2_E
            sp = _softplus2(z)
            if masked:
                sp = jnp.where(causal, sp, 0.0)
            suffix = _suffix_sum(sp, tri2_ref[...])
            c = c_ref[h, rows, :]
            a = jnp.exp2(z - sp - suffix - c)
            if masked:
                a = jnp.where(causal, a, 0.0)
            a_heads.append(a.astype(jnp.bfloat16))
            c_ref[h, rows, :] = c + suffix[:, 0:1] + sp[:, 0:1]
        acc_ref[rows, :] += _dot(jnp.concatenate(a_heads, axis=1), v2)

    def all_dead():
        return (jnp.min(c_ref[...]) >= DEAD_LOG2).astype(jnp.int32)

    @pl.when(qi == 0)
    def _():
        step(1, 1, True)
        step(0, 1, False)
        step(0, 0, True)

    @pl.when(qi > 0)
    def _():
        step(base + 1, 1, True)
        step(base, 1, False)
        step(base, 0, True)
        step(base - 1, 0, False)

        def body(carry):
            j, _ = carry
            step(base - 1 - j, 1, False)
            step(base - 2 - j, 0, False)
            return j + 1, all_dead()

        _, dead = lax.while_loop(lambda carry: (carry[0] < base - 1) & (carry[1] == 0),
                                 body, (jnp.int32(0), all_dead()))

        @pl.when(dead == 0)
        def _():
            step(0, 1, False)

    o_ref[0] = acc_ref[...].astype(o_ref.dtype)


def _attn(qkv, *, n_heads, tk=256):
    b, s, n = qkv.shape
    d_attn = n // 3
    hd = d_attn // n_heads
    tq = 2 * tk
    assert 2 * hd == V7X_LANES and s % tq == 0
    pairs = n_heads // 2
    tri = jnp.tril(jnp.ones((tk, tk), jnp.bfloat16), -1)
    return pl.pallas_call(
        functools.partial(_attn_body, tk=tk, hd=hd),
        out_shape=jax.ShapeDtypeStruct((b, s, d_attn), jnp.bfloat16),
        grid=(b, pairs, s // tq),
        in_specs=[
            pl.BlockSpec((1, tq, 2 * hd), lambda bi, p, i: (bi, i, p)),
            pl.BlockSpec((1, s, 2 * hd), lambda bi, p, i: (bi, 0, pairs + p)),
            pl.BlockSpec((1, s, 2 * hd), lambda bi, p, i: (bi, 0, 2 * pairs + p)),
            _resident((2 * tk, tk)),
        ],
        out_specs=pl.BlockSpec((1, tq, 2 * hd), lambda bi, p, i: (bi, i, p)),
        scratch_shapes=[pltpu.VMEM((2, tq, 2 * hd), jnp.bfloat16),
                        pltpu.VMEM((2, tq, 1), jnp.float32),
                        pltpu.VMEM((tq, 2 * hd), jnp.float32)],
        compiler_params=_params(("arbitrary", "arbitrary", "arbitrary")),
        name="attn",
    )(qkv, qkv, qkv, jnp.concatenate([tri, tri], axis=0))


def _proj_body(x_ref, a_ref, w_ref, o_ref):
    o_ref[...] = x_ref[...] + _dot(a_ref[...], w_ref[...])


def _proj(x, a, w, *, tm=512):
    t, d = x.shape
    k = a.shape[1]
    assert t % tm == 0
    return pl.pallas_call(
        _proj_body,
        out_shape=jax.ShapeDtypeStruct((t, d), jnp.float32),
        grid=(t // tm,),
        in_specs=[pl.BlockSpec((tm, d), lambda i: (i, 0)),
                  pl.BlockSpec((tm, k), lambda i: (i, 0)),
                  _resident((k, d))],
        out_specs=pl.BlockSpec((tm, d), lambda i: (i, 0)),
        compiler_params=_params(("arbitrary",)),
        name="proj",
    )(x, a, w.astype(jnp.bfloat16))


def kernel(x, norm_g, final_g, ffn_w_in, ffn_w_out, conv_w_pw1, conv_b_pw1, conv_w_dw, conv_b_dw,
           conv_ln_g, conv_ln_b, conv_w_pw2, conv_b_pw2, attn_w_qkv, attn_w_o):
    b, s, d = x.shape
    depth = norm_g.shape[0]
    x = x.reshape(b * s, d)
    for i in range(depth):
        g = norm_g[i]
        j = i // 2
        x = _ffn(x, g[0], ffn_w_in[i, 0], ffn_w_out[i, 0])
        if i % 2 == 0:
            x = _conv(x.reshape(b, s, d), g[1], conv_w_pw1[j], conv_b_pw1[j], conv_w_dw[j],
                      conv_b_dw[j], conv_ln_g[j], conv_ln_b[j], conv_w_pw2[j],
                      conv_b_pw2[j]).reshape(b * s, d)
        else:
            hd = attn_w_qkv.shape[2] // 3 // N_HEADS
            qkv = _qkv(x, g[1], attn_w_qkv[j], scale=hd ** -0.5)
            a = _attn(qkv.reshape(b, s, -1), n_heads=N_HEADS)
            x = _proj(x, a.reshape(b * s, -1), attn_w_o[j])
        last = i == depth - 1
        x = _ffn(x, g[2], ffn_w_in[i, 1], ffn_w_out[i, 1], final_g if last else None)
    return x.reshape(b, s, d)
```

```python
import functools

import jax
import jax.numpy as jnp
from jax import lax
from jax.experimental import pallas as pl
from jax.experimental.pallas import tpu as pltpu

N_HEADS = 16
CONV_WIDTH = 31
RMS_EPS = 1e-6
LN_EPS = 1e-5

V7X_LANES = 128
V7X_SUBLANES = 8
V7X_VMEM_BYTES = 64 * 1024 * 1024
VMEM_LIMIT_BYTES = 56 * 1024 * 1024

CONV_HALO = 32


def _params(semantics):
    return pltpu.CompilerParams(dimension_semantics=semantics,
                                vmem_limit_bytes=VMEM_LIMIT_BYTES)


def _resident(shape):
    zeros = (0,) * len(shape)
    return pl.BlockSpec(shape, lambda *_: zeros, pipeline_mode=pl.Buffered(1))


def _rms(x, g):
    return x * lax.rsqrt(jnp.mean(x * x, axis=-1, keepdims=True) + RMS_EPS) * g


def _dot(a, b):
    return jnp.dot(a, b, preferred_element_type=jnp.float32)


def _ffn_body(*refs, d_ff, chunk, pre_proj, final_norm):
    refs = list(refs)
    x_ref = refs.pop(0)
    a_ref, wa_ref = (refs.pop(0), refs.pop(0)) if pre_proj else (None, None)
    g_ref, win_ref, wout_ref = refs.pop(0), refs.pop(0), refs.pop(0)
    fg_ref = refs.pop(0) if final_norm else None
    o_ref, act_ref = refs
    x = x_ref[...]
    if pre_proj:
        x = x + _dot(a_ref[...], wa_ref[...])
    h = _rms(x, g_ref[...]).astype(jnp.bfloat16)
    for c in range(d_ff // chunk):
        lo = c * chunk
        gate = _dot(h, win_ref[:, lo:lo + chunk])
        up = _dot(h, win_ref[:, d_ff + lo:d_ff + lo + chunk])
        act_ref[:, lo:lo + chunk] = (gate * jax.nn.sigmoid(gate) * up).astype(jnp.bfloat16)
    y = x + 0.5 * _dot(act_ref[...], wout_ref[...])
    if final_norm:
        y = _rms(y, fg_ref[...])
    o_ref[...] = y


def _ffn(x, g, w_in, w_out, *, pre=None, final_g=None, tm=1024, chunk=256):
    t, d = x.shape
    d_ff = w_out.shape[0]
    assert t % tm == 0 and d_ff % chunk == 0
    final_norm = final_g is not None
    tile = lambda width: pl.BlockSpec((tm, width), lambda i: (i, 0))
    in_specs, args = [tile(d)], [x]
    if pre is not None:
        a, w_a = pre
        in_specs += [tile(a.shape[1]), _resident(w_a.shape)]
        args += [a, w_a.astype(jnp.bfloat16)]
    in_specs += [_resident((1, d)), _resident((d, 2 * d_ff)), _resident((d_ff, d))]
    args += [g.reshape(1, d), w_in.astype(jnp.bfloat16), w_out.astype(jnp.bfloat16)]
    if final_norm:
        in_specs.append(_resident((1, d)))
        args.append(final_g.reshape(1, d))
    return pl.pallas_call(
        functools.partial(_ffn_body, d_ff=d_ff, chunk=chunk, pre_proj=pre is not None,
                          final_norm=final_norm),
        out_shape=jax.ShapeDtypeStruct((t, d), jnp.float32),
        grid=(t // tm,),
        in_specs=in_specs,
        out_specs=pl.BlockSpec((tm, d), lambda i: (i, 0)),
        scratch_shapes=[pltpu.VMEM((tm, d_ff), jnp.bfloat16)],
        compiler_params=_params(("arbitrary",)),
        name="ffn",
    )(*args)


CONV_ROWS = 128
CONV_LANES = 128


def _dwconv_chunk(u_ref, wdw_ref, bdw_ref, y_ref, t0, l0):
    lanes = pl.ds(l0, CONV_LANES)
    first = CONV_HALO - (CONV_WIDTH - 1)
    acc = jnp.broadcast_to(bdw_ref[:, lanes], (CONV_ROWS, CONV_LANES))
    for r in range(V7X_SUBLANES):
        part = None
        for w in range(CONV_WIDTH):
            off = first + w
            if off % V7X_SUBLANES != r:
                continue
            slab = u_ref[pl.ds(pl.multiple_of(t0 + (off - r), V7X_SUBLANES),
                               CONV_ROWS + (V7X_SUBLANES if r else 0)), lanes]
            term = slab * wdw_ref[pl.ds(w, 1), lanes]
            part = term if part is None else part + term
        acc = acc + part[r:r + CONV_ROWS]
    y_ref[pl.ds(t0, CONV_ROWS), lanes] = acc


def _conv_body(x_ref, g_ref, w1_ref, b1_ref, wdw_ref, bdw_ref, lng_ref, lnb_ref,
               w2_ref, b2_ref, o_ref, u_ref, y_ref, *, ts):
    d = x_ref.shape[-1]

    @pl.when(pl.program_id(1) == 0)
    def _():
        u_ref[0:CONV_HALO, :] = jnp.zeros((CONV_HALO, d), jnp.float32)

    x = x_ref[0]
    h = _rms(x, g_ref[...]).astype(jnp.bfloat16)
    val = _dot(h, w1_ref[:, 0:d]) + b1_ref[:, 0:d]
    gate = _dot(h, w1_ref[:, d:2 * d]) + b1_ref[:, d:2 * d]
    u_ref[CONV_HALO:CONV_HALO + ts, :] = val * jax.nn.sigmoid(gate)

    def chunk(i, carry):
        _dwconv_chunk(u_ref, wdw_ref, bdw_ref, y_ref,
                      pl.multiple_of((i // (d // CONV_LANES)) * CONV_ROWS, CONV_ROWS),
                      pl.multiple_of((i % (d // CONV_LANES)) * CONV_LANES, CONV_LANES))
        return carry

    lax.fori_loop(0, (ts // CONV_ROWS) * (d // CONV_LANES), chunk, 0)
    u_ref[0:CONV_HALO, :] = u_ref[ts:ts + CONV_HALO, :]

    acc = y_ref[...]
    mu = jnp.mean(acc, axis=-1, keepdims=True)
    cen = acc - mu
    var = jnp.mean(cen * cen, axis=-1, keepdims=True)
    y = cen * lax.rsqrt(var + LN_EPS) * lng_ref[...] + lnb_ref[...]
    y = (y * jax.nn.sigmoid(y)).astype(jnp.bfloat16)
    o_ref[0] = x + _dot(y, w2_ref[...]) + b2_ref[...]


def _conv(x, g, w1, b1, wdw, bdw, lng, lnb, w2, b2, *, ts=512):
    b, s, d = x.shape
    assert s % ts == 0 and ts % CONV_ROWS == 0 and d % CONV_LANES == 0
    assert ts >= CONV_HALO and CONV_WIDTH - 1 <= CONV_HALO and CONV_HALO % V7X_SUBLANES == 0
    row = lambda a: a.reshape(1, -1)
    return pl.pallas_call(
        functools.partial(_conv_body, ts=ts),
        out_shape=jax.ShapeDtypeStruct((b, s, d), jnp.float32),
        grid=(b, s // ts),
        in_specs=[
            pl.BlockSpec((1, ts, d), lambda bi, si: (bi, si, 0)),
            _resident((1, d)),
            _resident((d, 2 * d)),
            _resident((1, 2 * d)),
            _resident((CONV_WIDTH, d)),
            _resident((1, d)),
            _resident((1, d)),
            _resident((1, d)),
            _resident((d, d)),
            _resident((1, d)),
        ],
        out_specs=pl.BlockSpec((1, ts, d), lambda bi, si: (bi, si, 0)),
        scratch_shapes=[pltpu.VMEM((CONV_HALO + ts, d), jnp.float32),
                        pltpu.VMEM((ts, d), jnp.float32)],
        compiler_params=_params(("arbitrary", "arbitrary")),
        name="conv",
    )(x, row(g), w1.astype(jnp.bfloat16), row(b1), wdw, row(bdw), row(lng), row(lnb),
      w2.astype(jnp.bfloat16), row(b2))


def _qkv_body(x_ref, g_ref, w_ref, o_ref, *, d_attn, scale):
    h = _rms(x_ref[...], g_ref[...]).astype(jnp.bfloat16)
    o_ref[:, 0:d_attn] = (_dot(h, w_ref[:, 0:d_attn]) * scale).astype(jnp.bfloat16)
    o_ref[:, d_attn:] = _dot(h, w_ref[:, d_attn:]).astype(jnp.bfloat16)


def _qkv(x, g, w, *, scale, tm=1024):
    t, d = x.shape
    n = w.shape[1]
    assert t % tm == 0
    return pl.pallas_call(
        functools.partial(_qkv_body, d_attn=n // 3, scale=scale),
        out_shape=jax.ShapeDtypeStruct((t, n), jnp.bfloat16),
        grid=(t // tm,),
        in_specs=[pl.BlockSpec((tm, d), lambda i: (i, 0)), _resident((1, d)), _resident((d, n))],
        out_specs=pl.BlockSpec((tm, n), lambda i: (i, 0)),
        compiler_params=_params(("arbitrary",)),
        name="qkv",
    )(x, g.reshape(1, d), w.astype(jnp.bfloat16))


LOG2_E = 1.4426950408889634

DEAD_LOG2 = 160.0

ATTN_BLOCK = 128
ATTN_DEPTH = 3


def _softplus2(y):
    neg_abs = pltpu.bitcast(pltpu.bitcast(y, jnp.uint32) | jnp.uint32(0x80000000), jnp.float32)
    return jnp.maximum(y, 0.0) + jnp.log2(1.0 + jnp.exp2(neg_abs))


def _suffix_sum(sp, tri):
    hi = sp.astype(jnp.bfloat16)
    lo = (sp - hi.astype(jnp.float32)).astype(jnp.bfloat16)
    return _dot(jnp.concatenate([hi, lo], axis=1), tri)


def _attn_body(q_ref, k_ref, v_ref, tri_ref, o_ref, c_ref, acc_ref, *, ns, hd):
    g, depth = ATTN_BLOCK, ATTN_DEPTH
    qi = pl.program_id(2)
    base = qi * ns
    lane = lax.broadcasted_iota(jnp.int32, (1, 2 * hd), 1)
    head_lanes = (lane < hd, lane >= hd)
    row = lax.broadcasted_iota(jnp.int32, (g, 2 * g), 0)
    col = lax.broadcasted_iota(jnp.int32, (g, 2 * g), 1)
    causal = (col & (g - 1)) < row
    nt = (((1,), (1,)), ((), ()))

    c_ref[...] = jnp.zeros_like(c_ref)
    acc_ref[...] = jnp.zeros_like(acc_ref)

    def mask_first(x):
        top = jnp.where(causal, x[:g], 0.0)
        return top if x.shape[0] == g else jnp.concatenate([top, x[g:]], axis=0)

    def per_head(x):
        return jnp.concatenate([jnp.where(hl, x, jnp.zeros_like(x)) for hl in head_lanes], axis=0)

    def scores(kb, r_lo, r_hi, masked):
        rows = pl.ds(r_lo * g, (r_hi - r_lo + 1) * g)
        k = k_ref[0, pl.ds(pl.multiple_of(kb * g, g), g), :]
        z = lax.dot_general(q_ref[0, rows, :], per_head(k), nt, preferred_element_type=jnp.float32)
        z = z * LOG2_E
        sp = _softplus2(z)
        if masked:
            sp = mask_first(sp)
        suffix = _suffix_sum(sp, tri_ref[...])
        totals = [suffix[:, h * g:h * g + 1] + sp[:, h * g:h * g + 1] for h in range(2)]
        return z - sp - suffix, totals

    def apply(kb, r_lo, r_hi, masked, w, totals):
        m = (r_hi - r_lo + 1) * g
        rows = pl.ds(r_lo * g, m)
        v = v_ref[0, pl.ds(pl.multiple_of(kb * g, g), g), :]
        c = [c_ref[h, rows, :] for h in range(2)]
        a = jnp.exp2(w - jnp.concatenate([jnp.broadcast_to(ch, (m, g)) for ch in c], axis=1))
        if masked:
            a = mask_first(a)
        acc_ref[rows, :] += _dot(a.astype(jnp.bfloat16), per_head(v))
        for h in range(2):
            c_ref[h, rows, :] = c[h] + totals[h]

    def step(*blk):
        apply(*blk, *scores(*blk))

    def dead(r_lo):
        return jnp.min(c_ref[:, r_lo * g:, :]) >= DEAD_LOG2

    def band(min_off):
        blks = [(base + off, max(off, 0), min(off + depth - 1, ns - 1), off >= 0)
                for off in range(ns - 1, max(-depth, min_off - 1), -1)]
        partial = [scores(*blk) for blk in blks]
        for blk, (w, totals) in zip(blks, partial):
            apply(*blk, w, totals)

    def ramp(min_off):
        offs = [off for off in range(ns - 1 - depth, -depth, -1) if off >= min_off]
        if offs:
            @pl.when(jnp.logical_not(dead(offs[-1] + depth)))
            def _():
                for off in offs:
                    step(base + off, off + depth, ns - 1, False)

    @pl.when(qi == 0)
    def _():
        band(0)
        ramp(0)

    @pl.when(qi > 0)
    def _():
        band(1 - depth)
        ramp(1 - depth)

        def body(carry):
            kb, _ = carry
            step(kb, 0, ns - 1, False)
            return kb - 1, dead(0).astype(jnp.int32)

        lax.while_loop(lambda carry: (carry[0] >= 0) & (carry[1] == 0), body,
                       (base - depth, dead(0).astype(jnp.int32)))

    o_ref[0] = acc_ref[...].astype(o_ref.dtype)


def _attn(qkv, *, n_heads, tq=1024):
    b, s, n = qkv.shape
    d_attn = n // 3
    hd = d_attn // n_heads
    g = ATTN_BLOCK
    ns = tq // g
    assert 2 * hd == V7X_LANES and s % tq == 0 and tq % g == 0 and ns >= ATTN_DEPTH - 1
    pairs = n_heads // 2
    tri = jnp.tril(jnp.ones((g, g), jnp.bfloat16), -1)
    tri = jnp.kron(jnp.eye(2, dtype=jnp.bfloat16), tri)
    return pl.pallas_call(
        functools.partial(_attn_body, ns=ns, hd=hd),
        out_shape=jax.ShapeDtypeStruct((b, s, d_attn), jnp.bfloat16),
        grid=(b, pairs, s // tq),
        in_specs=[
            pl.BlockSpec((1, tq, 2 * hd), lambda bi, p, i: (bi, i, p)),
            pl.BlockSpec((1, s, 2 * hd), lambda bi, p, i: (bi, 0, pairs + p)),
            pl.BlockSpec((1, s, 2 * hd), lambda bi, p, i: (bi, 0, 2 * pairs + p)),
            _resident((4 * g, 2 * g)),
        ],
        out_specs=pl.BlockSpec((1, tq, 2 * hd), lambda bi, p, i: (bi, i, p)),
        scratch_shapes=[pltpu.VMEM((2, tq, 1), jnp.float32),
                        pltpu.VMEM((tq, 2 * hd), jnp.float32)],
        compiler_params=_params(("arbitrary", "arbitrary", "arbitrary")),
        name="attn",
    )(qkv, qkv, qkv, jnp.concatenate([tri, tri], axis=0))


def kernel(x, norm_g, final_g, ffn_w_in, ffn_w_out, conv_w_pw1, conv_b_pw1, conv_w_dw, conv_b_dw,
           conv_ln_g, conv_ln_b, conv_w_pw2, conv_b_pw2, attn_w_qkv, attn_w_o):
    b, s, d = x.shape
    depth = norm_g.shape[0]
    x = x.reshape(b * s, d)
    for i in range(depth):
        g = norm_g[i]
        j = i // 2
        x = _ffn(x, g[0], ffn_w_in[i, 0], ffn_w_out[i, 0])
        pre = None
        if i % 2 == 0:
            x = _conv(x.reshape(b, s, d), g[1], conv_w_pw1[j], conv_b_pw1[j], conv_w_dw[j],
                      conv_b_dw[j], conv_ln_g[j], conv_ln_b[j], conv_w_pw2[j],
                      conv_b_pw2[j]).reshape(b * s, d)
        else:
            hd = attn_w_qkv.shape[2] // 3 // N_HEADS
            qkv = _qkv(x, g[1], attn_w_qkv[j], scale=hd ** -0.5)
            a = _attn(qkv.reshape(b, s, -1), n_heads=N_HEADS)
            pre = (a.reshape(b * s, -1), attn_w_o[j])
        x = _ffn(x, g[2], ffn_w_in[i, 1], ffn_w_out[i, 1], pre=pre,
                 final_g=final_g if i == depth - 1 else None)
    return x.reshape(b, s, d)
```

```python
import functools

import jax
import jax.numpy as jnp
from jax import lax
from jax.experimental import pallas as pl
from jax.experimental.pallas import tpu as pltpu

N_HEADS = 16
CONV_WIDTH = 31
RMS_EPS = 1e-6
LN_EPS = 1e-5

V7X_LANES = 128
V7X_SUBLANES = 8
V7X_VMEM_BYTES = 64 * 1024 * 1024
VMEM_LIMIT_BYTES = 56 * 1024 * 1024

CONV_HALO = 32


def _params(semantics):
    return pltpu.CompilerParams(dimension_semantics=semantics,
                                vmem_limit_bytes=VMEM_LIMIT_BYTES)


def _resident(shape):
    zeros = (0,) * len(shape)
    return pl.BlockSpec(shape, lambda *_: zeros, pipeline_mode=pl.Buffered(1))


def _rms(x, g):
    return x * lax.rsqrt(jnp.mean(x * x, axis=-1, keepdims=True) + RMS_EPS) * g


def _dot(a, b):
    return jnp.dot(a, b, preferred_element_type=jnp.float32)


def _ffn_body(*refs, d_ff, chunk, pre_proj, final_norm):
    refs = list(refs)
    x_ref = refs.pop(0)
    a_ref, wa_ref = (refs.pop(0), refs.pop(0)) if pre_proj else (None, None)
    g_ref, win_ref, wout_ref = refs.pop(0), refs.pop(0), refs.pop(0)
    fg_ref = refs.pop(0) if final_norm else None
    o_ref, act_ref = refs
    x = x_ref[...]
    if pre_proj:
        x = x + _dot(a_ref[...], wa_ref[...])
    h = _rms(x, g_ref[...]).astype(jnp.bfloat16)
    for c in range(d_ff // chunk):
        lo = c * chunk
        gate = _dot(h, win_ref[:, lo:lo + chunk])
        up = _dot(h, win_ref[:, d_ff + lo:d_ff + lo + chunk])
        act_ref[:, lo:lo + chunk] = (gate * jax.nn.sigmoid(gate) * up).astype(jnp.bfloat16)
    y = x + 0.5 * _dot(act_ref[...], wout_ref[...])
    if final_norm:
        y = _rms(y, fg_ref[...])
    o_ref[...] = y


def _ffn(x, g, w_in, w_out, *, pre=None, final_g=None, tm=1024, chunk=256):
    t, d = x.shape
    d_ff = w_out.shape[0]
    assert t % tm == 0 and d_ff % chunk == 0
    final_norm = final_g is not None
    tile = lambda width: pl.BlockSpec((tm, width), lambda i: (i, 0))
    in_specs, args = [tile(d)], [x]
    if pre is not None:
        a, w_a = pre
        in_specs += [tile(a.shape[1]), _resident(w_a.shape)]
        args += [a, w_a.astype(jnp.bfloat16)]
    in_specs += [_resident((1, d)), _resident((d, 2 * d_ff)), _resident((d_ff, d))]
    args += [g.reshape(1, d), w_in.astype(jnp.bfloat16), w_out.astype(jnp.bfloat16)]
    if final_norm:
        in_specs.append(_resident((1, d)))
        args.append(final_g.reshape(1, d))
    return pl.pallas_call(
        functools.partial(_ffn_body, d_ff=d_ff, chunk=chunk, pre_proj=pre is not None,
                          final_norm=final_norm),
        out_shape=jax.ShapeDtypeStruct((t, d), jnp.float32),
        grid=(t // tm,),
        in_specs=in_specs,
        out_specs=pl.BlockSpec((tm, d), lambda i: (i, 0)),
        scratch_shapes=[pltpu.VMEM((tm, d_ff), jnp.bfloat16)],
        compiler_params=_params(("arbitrary",)),
        name="ffn",
    )(*args)


CONV_ROWS = 128
CONV_LANES = 128


def _dwconv_chunk(u_ref, wdw_ref, bdw_ref, y_ref, t0, l0):
    lanes = pl.ds(l0, CONV_LANES)
    first = CONV_HALO - (CONV_WIDTH - 1)
    acc = jnp.broadcast_to(bdw_ref[:, lanes], (CONV_ROWS, CONV_LANES))
    for r in range(V7X_SUBLANES):
        part = None
        for w in range(CONV_WIDTH):
            off = first + w
            if off % V7X_SUBLANES != r:
                continue
            slab = u_ref[pl.ds(pl.multiple_of(t0 + (off - r), V7X_SUBLANES),
                               CONV_ROWS + (V7X_SUBLANES if r else 0)), lanes]
            term = slab * wdw_ref[pl.ds(w, 1), lanes]
            part = term if part is None else part + term
        acc = acc + part[r:r + CONV_ROWS]
    y_ref[pl.ds(t0, CONV_ROWS), lanes] = acc


def _conv_body(x_ref, g_ref, w1_ref, b1_ref, wdw_ref, bdw_ref, lng_ref, lnb_ref,
               w2_ref, b2_ref, o_ref, u_ref, y_ref, *, ts):
    d = x_ref.shape[-1]

    @pl.when(pl.program_id(1) == 0)
    def _():
        u_ref[0:CONV_HALO, :] = jnp.zeros((CONV_HALO, d), jnp.float32)

    x = x_ref[0]
    h = _rms(x, g_ref[...]).astype(jnp.bfloat16)
    val = _dot(h, w1_ref[:, 0:d]) + b1_ref[:, 0:d]
    gate = _dot(h, w1_ref[:, d:2 * d]) + b1_ref[:, d:2 * d]
    u_ref[CONV_HALO:CONV_HALO + ts, :] = val * jax.nn.sigmoid(gate)

    def chunk(i, carry):
        _dwconv_chunk(u_ref, wdw_ref, bdw_ref, y_ref,
                      pl.multiple_of((i // (d // CONV_LANES)) * CONV_ROWS, CONV_ROWS),
                      pl.multiple_of((i % (d // CONV_LANES)) * CONV_LANES, CONV_LANES))
        return carry

    lax.fori_loop(0, (ts // CONV_ROWS) * (d // CONV_LANES), chunk, 0)
    u_ref[0:CONV_HALO, :] = u_ref[ts:ts + CONV_HALO, :]

    acc = y_ref[...]
    mu = jnp.mean(acc, axis=-1, keepdims=True)
    cen = acc - mu
    var = jnp.mean(cen * cen, axis=-1, keepdims=True)
    y = cen * lax.rsqrt(var + LN_EPS) * lng_ref[...] + lnb_ref[...]
    y = (y * jax.nn.sigmoid(y)).astype(jnp.bfloat16)
    o_ref[0] = x + _dot(y, w2_ref[...]) + b2_ref[...]


def _conv(x, g, w1, b1, wdw, bdw, lng, lnb, w2, b2, *, ts=512):
    b, s, d = x.shape
    assert s % ts == 0 and ts % CONV_ROWS == 0 and d % CONV_LANES == 0
    assert ts >= CONV_HALO and CONV_WIDTH - 1 <= CONV_HALO and CONV_HALO % V7X_SUBLANES == 0
    row = lambda a: a.reshape(1, -1)
    return pl.pallas_call(
        functools.partial(_conv_body, ts=ts),
        out_shape=jax.ShapeDtypeStruct((b, s, d), jnp.float32),
        grid=(b, s // ts),
        in_specs=[
            pl.BlockSpec((1, ts, d), lambda bi, si: (bi, si, 0)),
            _resident((1, d)),
            _resident((d, 2 * d)),
            _resident((1, 2 * d)),
            _resident((CONV_WIDTH, d)),
            _resident((1, d)),
            _resident((1, d)),
            _resident((1, d)),
            _resident((d, d)),
            _resident((1, d)),
        ],
        out_specs=pl.BlockSpec((1, ts, d), lambda bi, si: (bi, si, 0)),
        scratch_shapes=[pltpu.VMEM((CONV_HALO + ts, d), jnp.float32),
                        pltpu.VMEM((ts, d), jnp.float32)],
        compiler_params=_params(("arbitrary", "arbitrary")),
        name="conv",
    )(x, row(g), w1.astype(jnp.bfloat16), row(b1), wdw, row(bdw), row(lng), row(lnb),
      w2.astype(jnp.bfloat16), row(b2))


def _qkv_body(x_ref, g_ref, w_ref, o_ref, *, d_attn, scale):
    h = _rms(x_ref[...], g_ref[...]).astype(jnp.bfloat16)
    o_ref[:, 0:d_attn] = (_dot(h, w_ref[:, 0:d_attn]) * scale).astype(jnp.bfloat16)
    o_ref[:, d_attn:] = _dot(h, w_ref[:, d_attn:]).astype(jnp.bfloat16)


def _qkv(x, g, w, *, scale, tm=1024):
    t, d = x.shape
    n = w.shape[1]
    assert t % tm == 0
    return pl.pallas_call(
        functools.partial(_qkv_body, d_attn=n // 3, scale=scale),
        out_shape=jax.ShapeDtypeStruct((t, n), jnp.bfloat16),
        grid=(t // tm,),
        in_specs=[pl.BlockSpec((tm, d), lambda i: (i, 0)), _resident((1, d)), _resident((d, n))],
        out_specs=pl.BlockSpec((tm, n), lambda i: (i, 0)),
        compiler_params=_params(("arbitrary",)),
        name="qkv",
    )(x, g.reshape(1, d), w.astype(jnp.bfloat16))


LOG2_E = 1.4426950408889634

DEAD_LOG2 = 160.0

MASKED_SCORE = -1e30

ATTN_BLOCK = 128
ATTN_DEPTH = 3


def _softplus2(y):
    neg_abs = pltpu.bitcast(pltpu.bitcast(y, jnp.uint32) | jnp.uint32(0x80000000), jnp.float32)
    return jnp.maximum(y, 0.0) + jnp.log2(1.0 + jnp.exp2(neg_abs))


def _suffix_sum(sp, tri):
    return _dot(sp.astype(jnp.bfloat16), tri)


def _attn_body(q_ref, k_ref, v_ref, tri_ref, o_ref, c_ref, acc_ref, *, ns, hd):
    g, depth = ATTN_BLOCK, ATTN_DEPTH
    qi = pl.program_id(2)
    base = qi * ns
    lane = lax.broadcasted_iota(jnp.int32, (1, 2 * hd), 1)
    head_lanes = (lane < hd, lane >= hd)
    row = lax.broadcasted_iota(jnp.int32, (g, 2 * g), 0)
    col = lax.broadcasted_iota(jnp.int32, (g, 2 * g), 1)
    causal = (col & (g - 1)) < row
    nt = (((1,), (1,)), ((), ()))

    c_ref[...] = jnp.zeros_like(c_ref)
    acc_ref[...] = jnp.zeros_like(acc_ref)

    def mask_first(z):
        top = jnp.where(causal, z[:g], MASKED_SCORE)
        return top if z.shape[0] == g else jnp.concatenate([top, z[g:]], axis=0)

    def per_head(x):
        return jnp.concatenate([jnp.where(hl, x, jnp.zeros_like(x)) for hl in head_lanes], axis=0)

    def scores(kb, r_lo, r_hi, masked):
        rows = pl.ds(r_lo * g, (r_hi - r_lo + 1) * g)
        k = k_ref[0, pl.ds(pl.multiple_of(kb * g, g), g), :]
        z = lax.dot_general(q_ref[0, rows, :], per_head(k), nt, preferred_element_type=jnp.float32)
        z = z * LOG2_E
        if masked:
            z = mask_first(z)
        sp = _softplus2(z)
        suffix = _suffix_sum(sp, tri_ref[...])
        totals = [suffix[:, h * g:h * g + 1] + sp[:, h * g:h * g + 1] for h in range(2)]
        return z - sp - suffix, totals

    def apply(kb, r_lo, r_hi, w, totals):
        m = (r_hi - r_lo + 1) * g
        rows = pl.ds(r_lo * g, m)
        v = v_ref[0, pl.ds(pl.multiple_of(kb * g, g), g), :]
        c = [c_ref[h, rows, :] for h in range(2)]
        a = jnp.exp2(w - jnp.concatenate([jnp.broadcast_to(ch, (m, g)) for ch in c], axis=1))
        acc_ref[rows, :] += _dot(a.astype(jnp.bfloat16), per_head(v))
        for h in range(2):
            c_ref[h, rows, :] = c[h] + totals[h]

    def step(*blk):
        apply(*blk[:3], *scores(*blk))

    def dead(r_lo):
        return jnp.min(c_ref[:, r_lo * g:, :]) >= DEAD_LOG2

    def band(min_off):
        blks = [(base + off, max(off, 0), min(off + depth - 1, ns - 1), off >= 0)
                for off in range(ns - 1, max(-depth, min_off - 1), -1)]
        partial = [scores(*blk) for blk in blks]
        for blk, (w, totals) in zip(blks, partial):
            apply(*blk[:3], w, totals)

    def ramp(min_off):
        offs = [off for off in range(ns - 1 - depth, -depth, -1) if off >= min_off]
        if offs:
            @pl.when(jnp.logical_not(dead(offs[-1] + depth)))
            def _():
                for off in offs:
                    step(base + off, off + depth, ns - 1, False)

    @pl.when(qi == 0)
    def _():
        band(0)
        ramp(0)

    @pl.when(qi > 0)
    def _():
        band(1 - depth)
        ramp(1 - depth)

        def body(carry):
            kb, _ = carry
            step(kb, 0, ns - 1, False)
            return kb - 1, dead(0).astype(jnp.int32)

        lax.while_loop(lambda carry: (carry[0] >= 0) & (carry[1] == 0), body,
                       (base - depth, dead(0).astype(jnp.int32)))

    o_ref[0] = acc_ref[...].astype(o_ref.dtype)


def _attn(qkv, *, n_heads, tq=1024):
    b, s, n = qkv.shape
    d_attn = n // 3
    hd = d_attn // n_heads
    g = ATTN_BLOCK
    ns = tq // g
    assert 2 * hd == V7X_LANES and s % tq == 0 and tq % g == 0 and ns >= ATTN_DEPTH - 1
    pairs = n_heads // 2
    tri = jnp.tril(jnp.ones((g, g), jnp.bfloat16), -1)
    tri = jnp.kron(jnp.eye(2, dtype=jnp.bfloat16), tri)
    return pl.pallas_call(
        functools.partial(_attn_body, ns=ns, hd=hd),
        out_shape=jax.ShapeDtypeStruct((b, s, d_attn), jnp.bfloat16),
        grid=(b, pairs, s // tq),
        in_specs=[
            pl.BlockSpec((1, tq, 2 * hd), lambda bi, p, i: (bi, i, p)),
            pl.BlockSpec((1, s, 2 * hd), lambda bi, p, i: (bi, 0, pairs + p)),
            pl.BlockSpec((1, s, 2 * hd), lambda bi, p, i: (bi, 0, 2 * pairs + p)),
            _resident((2 * g, 2 * g)),
        ],
        out_specs=pl.BlockSpec((1, tq, 2 * hd), lambda bi, p, i: (bi, i, p)),
        scratch_shapes=[pltpu.VMEM((2, tq, 1), jnp.float32),
                        pltpu.VMEM((tq, 2 * hd), jnp.float32)],
        compiler_params=_params(("arbitrary", "arbitrary", "arbitrary")),
        name="attn",
    )(qkv, qkv, qkv, tri)


def kernel(x, norm_g, final_g, ffn_w_in, ffn_w_out, conv_w_pw1, conv_b_pw1, conv_w_dw, conv_b_dw,
           conv_ln_g, conv_ln_b, conv_w_pw2, conv_b_pw2, attn_w_qkv, attn_w_o):
    b, s, d = x.shape
    depth = norm_g.shape[0]
    x = x.reshape(b * s, d)
    for i in range(depth):
        g = norm_g[i]
        j = i // 2
        x = _ffn(x, g[0], ffn_w_in[i, 0], ffn_w_out[i, 0])
        pre = None
        if i % 2 == 0:
            x = _conv(x.reshape(b, s, d), g[1], conv_w_pw1[j], conv_b_pw1[j], conv_w_dw[j],
                      conv_b_dw[j], conv_ln_g[j], conv_ln_b[j], conv_w_pw2[j],
                      conv_b_pw2[j]).reshape(b * s, d)
        else:
            hd = attn_w_qkv.shape[2] // 3 // N_HEADS
            qkv = _qkv(x, g[1], attn_w_qkv[j], scale=hd ** -0.5)
            a = _attn(qkv.reshape(b, s, -1), n_heads=N_HEADS)
            pre = (a.reshape(b * s, -1), attn_w_o[j])
        x = _ffn(x, g[2], ffn_w_in[i, 1], ffn_w_out[i, 1], pre=pre,
                 final_g=final_g if i == depth - 1 else None)
    return x.reshape(b, s, d)
```

```python
import functools

import jax
import jax.numpy as jnp
from jax import lax
from jax.experimental import pallas as pl
from jax.experimental.pallas import tpu as pltpu

N_HEADS = 16
CONV_WIDTH = 31
RMS_EPS = 1e-6
LN_EPS = 1e-5

V7X_LANES = 128
V7X_SUBLANES = 8
V7X_VMEM_BYTES = 64 * 1024 * 1024
VMEM_LIMIT_BYTES = 56 * 1024 * 1024

CONV_HALO = 32


def _params(semantics):
    return pltpu.CompilerParams(dimension_semantics=semantics,
                                vmem_limit_bytes=VMEM_LIMIT_BYTES)


def _resident(shape):
    zeros = (0,) * len(shape)
    return pl.BlockSpec(shape, lambda *_: zeros, pipeline_mode=pl.Buffered(1))


def _rms(x, g):
    return x * lax.rsqrt(jnp.mean(x * x, axis=-1, keepdims=True) + RMS_EPS) * g


def _dot(a, b):
    return jnp.dot(a, b, preferred_element_type=jnp.float32)


STAGE_ROWS_IN = 64
STAGE_ROWS_OUT = 256


def _stage_weights(src_hbm, dst_ref, stage_ref, sem):
    rows = stage_ref.shape[1]
    n = src_hbm.shape[0] // rows

    def copy(i):
        return pltpu.make_async_copy(src_hbm.at[pl.ds(i * rows, rows), :], stage_ref.at[i % 2],
                                     sem.at[i % 2])

    copy(0).start()
    for i in range(n):
        if i + 1 < n:
            copy(i + 1).start()
        copy(i).wait()
        dst_ref[pl.ds(i * rows, rows), :] = stage_ref[i % 2].astype(dst_ref.dtype)


def _ffn_body(*refs, d_ff, chunk, which, pre_proj, final_norm):
    refs = list(refs)
    x_ref = refs.pop(0)
    a_ref, wa_ref = (refs.pop(0), refs.pop(0)) if pre_proj else (None, None)
    g_ref, win_hbm, wout_hbm = refs.pop(0), refs.pop(0), refs.pop(0)
    fg_ref = refs.pop(0) if final_norm else None
    o_ref, act_ref, win_ref, wout_ref, stage_in, stage_out, sem_in, sem_out = refs

    @pl.when(pl.program_id(0) == 0)
    def _():
        _stage_weights(win_hbm.at[which], win_ref, stage_in, sem_in)
        _stage_weights(wout_hbm.at[which], wout_ref, stage_out, sem_out)

    x = x_ref[...]
    if pre_proj:
        x = x + _dot(a_ref[...], wa_ref[...])
    h = _rms(x, g_ref[...]).astype(jnp.bfloat16)
    for c in range(d_ff // chunk):
        lo = c * chunk
        gate = _dot(h, win_ref[:, lo:lo + chunk])
        up = _dot(h, win_ref[:, d_ff + lo:d_ff + lo + chunk])
        act_ref[:, lo:lo + chunk] = (gate * jax.nn.sigmoid(gate) * up).astype(jnp.bfloat16)
    y = x + 0.5 * _dot(act_ref[...], wout_ref[...])
    if final_norm:
        y = _rms(y, fg_ref[...])
    o_ref[...] = y


def _ffn(x, g, w_in, w_out, which, *, pre=None, final_g=None, tm=1024, chunk=256):
    t, d = x.shape
    d_ff = w_out.shape[-2]
    assert t % tm == 0 and d_ff % chunk == 0
    assert d % STAGE_ROWS_IN == 0 and d_ff % STAGE_ROWS_OUT == 0
    final_norm = final_g is not None
    tile = lambda width: pl.BlockSpec((tm, width), lambda i: (i, 0))
    in_specs, args = [tile(d)], [x]
    if pre is not None:
        a, w_a = pre
        in_specs += [tile(a.shape[1]), _resident(w_a.shape)]
        args += [a, w_a.astype(jnp.bfloat16)]
    hbm = pl.BlockSpec(memory_space=pl.ANY)
    in_specs += [_resident((1, d)), hbm, hbm]
    args += [g.reshape(1, d), w_in, w_out]
    if final_norm:
        in_specs.append(_resident((1, d)))
        args.append(final_g.reshape(1, d))
    return pl.pallas_call(
        functools.partial(_ffn_body, d_ff=d_ff, chunk=chunk, which=which,
                          pre_proj=pre is not None, final_norm=final_norm),
        out_shape=jax.ShapeDtypeStruct((t, d), jnp.float32),
        grid=(t // tm,),
        in_specs=in_specs,
        out_specs=pl.BlockSpec((tm, d), lambda i: (i, 0)),
        scratch_shapes=[pltpu.VMEM((tm, d_ff), jnp.bfloat16),
                        pltpu.VMEM((d, 2 * d_ff), jnp.bfloat16),
                        pltpu.VMEM((d_ff, d), jnp.bfloat16),
                        pltpu.VMEM((2, STAGE_ROWS_IN, 2 * d_ff), jnp.float32),
                        pltpu.VMEM((2, STAGE_ROWS_OUT, d), jnp.float32),
                        pltpu.SemaphoreType.DMA((2,)),
                        pltpu.SemaphoreType.DMA((2,))],
        compiler_params=_params(("arbitrary",)),
        name="ffn",
    )(*args)


CONV_ROWS = 128
CONV_LANES = 128


def _dwconv_chunk(u_ref, wdw_ref, bdw_ref, y_ref, t0, l0):
    lanes = pl.ds(l0, CONV_LANES)
    first = CONV_HALO - (CONV_WIDTH - 1)
    acc = jnp.broadcast_to(bdw_ref[:, lanes], (CONV_ROWS, CONV_LANES))
    for r in range(V7X_SUBLANES):
        part = None
        for w in range(CONV_WIDTH):
            off = first + w
            if off % V7X_SUBLANES != r:
                continue
            slab = u_ref[pl.ds(pl.multiple_of(t0 + (off - r), V7X_SUBLANES),
                               CONV_ROWS + (V7X_SUBLANES if r else 0)), lanes]
            term = slab * wdw_ref[pl.ds(w, 1), lanes]
            part = term if part is None else part + term
        acc = acc + part[r:r + CONV_ROWS]
    y_ref[pl.ds(t0, CONV_ROWS), lanes] = acc


def _conv_body(x_ref, g_ref, w1_ref, b1_ref, wdw_ref, bdw_ref, lng_ref, lnb_ref,
               w2_ref, b2_ref, o_ref, u_ref, y_ref, *, ts):
    d = x_ref.shape[-1]

    @pl.when(pl.program_id(1) == 0)
    def _():
        u_ref[0:CONV_HALO, :] = jnp.zeros((CONV_HALO, d), jnp.float32)

    x = x_ref[0]
    h = _rms(x, g_ref[...]).astype(jnp.bfloat16)
    val = _dot(h, w1_ref[:, 0:d]) + b1_ref[:, 0:d]
    gate = _dot(h, w1_ref[:, d:2 * d]) + b1_ref[:, d:2 * d]
    u_ref[CONV_HALO:CONV_HALO + ts, :] = val * jax.nn.sigmoid(gate)

    def chunk(i, carry):
        _dwconv_chunk(u_ref, wdw_ref, bdw_ref, y_ref,
                      pl.multiple_of((i // (d // CONV_LANES)) * CONV_ROWS, CONV_ROWS),
                      pl.multiple_of((i % (d // CONV_LANES)) * CONV_LANES, CONV_LANES))
        return carry

    lax.fori_loop(0, (ts // CONV_ROWS) * (d // CONV_LANES), chunk, 0)
    u_ref[0:CONV_HALO, :] = u_ref[ts:ts + CONV_HALO, :]

    acc = y_ref[...]
    mu = jnp.mean(acc, axis=-1, keepdims=True)
    cen = acc - mu
    var = jnp.mean(cen * cen, axis=-1, keepdims=True)
    y = cen * lax.rsqrt(var + LN_EPS) * lng_ref[...] + lnb_ref[...]
    y = (y * jax.nn.sigmoid(y)).astype(jnp.bfloat16)
    o_ref[0] = x + _dot(y, w2_ref[...]) + b2_ref[...]


def _conv(x, g, w1, b1, wdw, bdw, lng, lnb, w2, b2, *, ts=512):
    b, s, d = x.shape
    assert s % ts == 0 and ts % CONV_ROWS == 0 and d % CONV_LANES == 0
    assert ts >= CONV_HALO and CONV_WIDTH - 1 <= CONV_HALO and CONV_HALO % V7X_SUBLANES == 0
    row = lambda a: a.reshape(1, -1)
    return pl.pallas_call(
        functools.partial(_conv_body, ts=ts),
        out_shape=jax.ShapeDtypeStruct((b, s, d), jnp.float32),
        grid=(b, s // ts),
        in_specs=[
            pl.BlockSpec((1, ts, d), lambda bi, si: (bi, si, 0)),
            _resident((1, d)),
            _resident((d, 2 * d)),
            _resident((1, 2 * d)),
            _resident((CONV_WIDTH, d)),
            _resident((1, d)),
            _resident((1, d)),
            _resident((1, d)),
            _resident((d, d)),
            _resident((1, d)),
        ],
        out_specs=pl.BlockSpec((1, ts, d), lambda bi, si: (bi, si, 0)),
        scratch_shapes=[pltpu.VMEM((CONV_HALO + ts, d), jnp.float32),
                        pltpu.VMEM((ts, d), jnp.float32)],
        compiler_params=_params(("arbitrary", "arbitrary")),
        name="conv",
    )(x, row(g), w1.astype(jnp.bfloat16), row(b1), wdw, row(bdw), row(lng), row(lnb),
      w2.astype(jnp.bfloat16), row(b2))


def _qkv_body(x_ref, g_ref, w_ref, o_ref, *, d_attn, scale):
    h = _rms(x_ref[...], g_ref[...]).astype(jnp.bfloat16)
    o_ref[:, 0:d_attn] = (_dot(h, w_ref[:, 0:d_attn]) * scale).astype(jnp.bfloat16)
    o_ref[:, d_attn:] = _dot(h, w_ref[:, d_attn:]).astype(jnp.bfloat16)


def _qkv(x, g, w, *, scale, tm=1024):
    t, d = x.shape
    n = w.shape[1]
    assert t % tm == 0
    return pl.pallas_call(
        functools.partial(_qkv_body, d_attn=n // 3, scale=scale),
        out_shape=jax.ShapeDtypeStruct((t, n), jnp.bfloat16),
        grid=(t // tm,),
        in_specs=[pl.BlockSpec((tm, d), lambda i: (i, 0)), _resident((1, d)), _resident((d, n))],
        out_specs=pl.BlockSpec((tm, n), lambda i: (i, 0)),
        compiler_params=_params(("arbitrary",)),
        name="qkv",
    )(x, g.reshape(1, d), w.astype(jnp.bfloat16))


LOG2_E = 1.4426950408889634

DEAD_LOG2 = 160.0

MASKED_SCORE = -1e30

ATTN_BLOCK = 128
ATTN_DEPTH = 3


def _softplus2(y):
    neg_abs = pltpu.bitcast(pltpu.bitcast(y, jnp.uint32) | jnp.uint32(0x80000000), jnp.float32)
    return jnp.maximum(y, 0.0) + jnp.log2(1.0 + jnp.exp2(neg_abs))


def _suffix_sum(sp, tri):
    return _dot(sp.astype(jnp.bfloat16), tri)


def _attn_body(q_ref, k_ref, v_ref, tri_ref, o_ref, c_ref, acc_ref, *, ns, hd):
    g, depth = ATTN_BLOCK, ATTN_DEPTH
    qi = pl.program_id(2)
    base = qi * ns
    lane = lax.broadcasted_iota(jnp.int32, (1, 2 * hd), 1)
    head_lanes = (lane < hd, lane >= hd)
    row = lax.broadcasted_iota(jnp.int32, (g, 2 * g), 0)
    col = lax.broadcasted_iota(jnp.int32, (g, 2 * g), 1)
    causal = (col & (g - 1)) < row
    nt = (((1,), (1,)), ((), ()))

    c_ref[...] = jnp.zeros_like(c_ref)
    acc_ref[...] = jnp.zeros_like(acc_ref)

    def mask_first(z):
        top = jnp.where(causal, z[:g], MASKED_SCORE)
        return top if z.shape[0] == g else jnp.concatenate([top, z[g:]], axis=0)

    def per_head(x):
        return jnp.concatenate([jnp.where(hl, x, jnp.zeros_like(x)) for hl in head_lanes], axis=0)

    def scores(kb, r_lo, r_hi, masked):
        rows = pl.ds(r_lo * g, (r_hi - r_lo + 1) * g)
        k = k_ref[0, pl.ds(pl.multiple_of(kb * g, g), g), :]
        z = lax.dot_general(q_ref[0, rows, :], per_head(k), nt, preferred_element_type=jnp.float32)
        if masked:
            z = mask_first(z)
        sp = _softplus2(z)
        suffix = _suffix_sum(sp, tri_ref[...])
        totals = [suffix[:, h * g:h * g + 1] + sp[:, h * g:h * g + 1] for h in range(2)]
        return z - sp - suffix, totals

    def apply(kb, r_lo, r_hi, w, totals):
        m = (r_hi - r_lo + 1) * g
        rows = pl.ds(r_lo * g, m)
        v = v_ref[0, pl.ds(pl.multiple_of(kb * g, g), g), :]
        c = [c_ref[h, rows, :] for h in range(2)]
        a = jnp.exp2(w - jnp.concatenate([jnp.broadcast_to(ch, (m, g)) for ch in c], axis=1))
        acc_ref[rows, :] += _dot(a.astype(jnp.bfloat16), per_head(v))
        for h in range(2):
            c_ref[h, rows, :] = c[h] + totals[h]

    def step(*blk):
        apply(*blk[:3], *scores(*blk))

    def dead(r_lo):
        return jnp.min(c_ref[:, r_lo * g:, :]) >= DEAD_LOG2

    def band(min_off):
        blks = [(base + off, max(off, 0), min(off + depth - 1, ns - 1), off >= 0)
                for off in range(ns - 1, max(-depth, min_off - 1), -1)]
        partial = [scores(*blk) for blk in blks]
        for blk, (w, totals) in zip(blks, partial):
            apply(*blk[:3], w, totals)

    def ramp(min_off):
        offs = [off for off in range(ns - 1 - depth, -depth, -1) if off >= min_off]
        if offs:
            @pl.when(jnp.logical_not(dead(offs[-1] + depth)))
            def _():
                for off in offs:
                    step(base + off, off + depth, ns - 1, False)

    @pl.when(qi == 0)
    def _():
        band(0)
        ramp(0)

    @pl.when(qi > 0)
    def _():
        band(1 - depth)
        ramp(1 - depth)

        def body(carry):
            kb, _ = carry
            step(kb, 0, ns - 1, False)
            return kb - 1, dead(0).astype(jnp.int32)

        lax.while_loop(lambda carry: (carry[0] >= 0) & (carry[1] == 0), body,
                       (base - depth, dead(0).astype(jnp.int32)))

    o_ref[0] = acc_ref[...].astype(o_ref.dtype)


def _attn(qkv, *, n_heads, tq=1024):
    b, s, n = qkv.shape
    d_attn = n // 3
    hd = d_attn // n_heads
    g = ATTN_BLOCK
    ns = tq // g
    assert 2 * hd == V7X_LANES and s % tq == 0 and tq % g == 0 and ns >= ATTN_DEPTH - 1
    pairs = n_heads // 2
    tri = jnp.tril(jnp.ones((g, g), jnp.bfloat16), -1)
    tri = jnp.kron(jnp.eye(2, dtype=jnp.bfloat16), tri)
    return pl.pallas_call(
        functools.partial(_attn_body, ns=ns, hd=hd),
        out_shape=jax.ShapeDtypeStruct((b, s, d_attn), jnp.bfloat16),
        grid=(b, pairs, s // tq),
        in_specs=[
            pl.BlockSpec((1, tq, 2 * hd), lambda bi, p, i: (bi, i, p)),
            pl.BlockSpec((1, s, 2 * hd), lambda bi, p, i: (bi, 0, pairs + p)),
            pl.BlockSpec((1, s, 2 * hd), lambda bi, p, i: (bi, 0, 2 * pairs + p)),
            _resident((2 * g, 2 * g)),
        ],
        out_specs=pl.BlockSpec((1, tq, 2 * hd), lambda bi, p, i: (bi, i, p)),
        scratch_shapes=[pltpu.VMEM((2, tq, 1), jnp.float32),
                        pltpu.VMEM((tq, 2 * hd), jnp.float32)],
        compiler_params=_params(("arbitrary", "arbitrary", "arbitrary")),
        name="attn",
    )(qkv, qkv, qkv, tri)


def kernel(x, norm_g, final_g, ffn_w_in, ffn_w_out, conv_w_pw1, conv_b_pw1, conv_w_dw, conv_b_dw,
           conv_ln_g, conv_ln_b, conv_w_pw2, conv_b_pw2, attn_w_qkv, attn_w_o):
    b, s, d = x.shape
    depth = norm_g.shape[0]
    x = x.reshape(b * s, d)
    for i in range(depth):
        g = norm_g[i]
        j = i // 2
        x = _ffn(x, g[0], ffn_w_in, ffn_w_out, (i, 0))
        pre = None
        if i % 2 == 0:
            x = _conv(x.reshape(b, s, d), g[1], conv_w_pw1[j], conv_b_pw1[j], conv_w_dw[j],
                      conv_b_dw[j], conv_ln_g[j], conv_ln_b[j], conv_w_pw2[j],
                      conv_b_pw2[j]).reshape(b * s, d)
        else:
            hd = attn_w_qkv.shape[2] // 3 // N_HEADS
            qkv = _qkv(x, g[1], attn_w_qkv[j], scale=hd ** -0.5 * LOG2_E)
            a = _attn(qkv.reshape(b, s, -1), n_heads=N_HEADS)
            pre = (a.reshape(b * s, -1), attn_w_o[j])
        x = _ffn(x, g[2], ffn_w_in, ffn_w_out, (i, 1), pre=pre,
                 final_g=final_g if i == depth - 1 else None)
    return x.reshape(b, s, d)
```

```python
import functools

import jax
import jax.numpy as jnp
from jax import lax
from jax.experimental import pallas as pl
from jax.experimental.pallas import tpu as pltpu

N_HEADS = 16
CONV_WIDTH = 31
RMS_EPS = 1e-6
LN_EPS = 1e-5

V7X_LANES = 128
V7X_SUBLANES = 8
V7X_VMEM_BYTES = 64 * 1024 * 1024
VMEM_LIMIT_BYTES = 56 * 1024 * 1024

CONV_HALO = 32


def _params(semantics):
    return pltpu.CompilerParams(dimension_semantics=semantics,
                                vmem_limit_bytes=VMEM_LIMIT_BYTES)


def _resident(shape):
    zeros = (0,) * len(shape)
    return pl.BlockSpec(shape, lambda *_: zeros, pipeline_mode=pl.Buffered(1))


def _rms(x, g):
    return x * lax.rsqrt(jnp.mean(x * x, axis=-1, keepdims=True) + RMS_EPS) * g


def _dot(a, b):
    return jnp.dot(a, b, preferred_element_type=jnp.float32)


STAGE_ROWS_IN = 64
STAGE_ROWS_OUT = 256


def _stage_weights(src_hbm, dst_ref, stage_ref, sem):
    rows = stage_ref.shape[1]
    n = src_hbm.shape[0] // rows

    def copy(i):
        return pltpu.make_async_copy(src_hbm.at[pl.ds(i * rows, rows), :], stage_ref.at[i % 2],
                                     sem.at[i % 2])

    copy(0).start()
    for i in range(n):
        if i + 1 < n:
            copy(i + 1).start()
        copy(i).wait()
        dst_ref[pl.ds(i * rows, rows), :] = stage_ref[i % 2].astype(dst_ref.dtype)


def _ffn_body(*refs, d_ff, chunk, which, pre_proj, final_norm):
    refs = list(refs)
    x_ref = refs.pop(0)
    a_ref, wa_ref = (refs.pop(0), refs.pop(0)) if pre_proj else (None, None)
    g_ref, win_hbm, wout_hbm = refs.pop(0), refs.pop(0), refs.pop(0)
    fg_ref = refs.pop(0) if final_norm else None
    o_ref, act_ref, win_ref, wout_ref, stage_in, stage_out, sem_in, sem_out = refs

    @pl.when(pl.program_id(0) == 0)
    def _():
        _stage_weights(win_hbm.at[which], win_ref, stage_in, sem_in)
        _stage_weights(wout_hbm.at[which], wout_ref, stage_out, sem_out)

    x = x_ref[...]
    if pre_proj:
        x = x + _dot(a_ref[...], wa_ref[...])
    h = _rms(x, g_ref[...]).astype(jnp.bfloat16)
    for c in range(d_ff // chunk):
        lo = c * chunk
        gate = _dot(h, win_ref[:, lo:lo + chunk])
        up = _dot(h, win_ref[:, d_ff + lo:d_ff + lo + chunk])
        act_ref[:, lo:lo + chunk] = (gate * jax.nn.sigmoid(gate) * up).astype(jnp.bfloat16)
    y = x + 0.5 * _dot(act_ref[...], wout_ref[...])
    if final_norm:
        y = _rms(y, fg_ref[...])
    o_ref[...] = y


def _ffn(x, g, w_in, w_out, which, *, pre=None, final_g=None, tm=1024, chunk=256):
    t, d = x.shape
    d_ff = w_out.shape[-2]
    assert t % tm == 0 and d_ff % chunk == 0
    assert d % STAGE_ROWS_IN == 0 and d_ff % STAGE_ROWS_OUT == 0
    final_norm = final_g is not None
    tile = lambda width: pl.BlockSpec((tm, width), lambda i: (i, 0))
    in_specs, args = [tile(d)], [x]
    if pre is not None:
        a, w_a = pre
        in_specs += [tile(a.shape[1]), _resident(w_a.shape)]
        args += [a, w_a.astype(jnp.bfloat16)]
    hbm = pl.BlockSpec(memory_space=pl.ANY)
    in_specs += [_resident((1, d)), hbm, hbm]
    args += [g.reshape(1, d), w_in, w_out]
    if final_norm:
        in_specs.append(_resident((1, d)))
        args.append(final_g.reshape(1, d))
    return pl.pallas_call(
        functools.partial(_ffn_body, d_ff=d_ff, chunk=chunk, which=which,
                          pre_proj=pre is not None, final_norm=final_norm),
        out_shape=jax.ShapeDtypeStruct((t, d), jnp.float32),
        grid=(t // tm,),
        in_specs=in_specs,
        out_specs=pl.BlockSpec((tm, d), lambda i: (i, 0)),
        scratch_shapes=[pltpu.VMEM((tm, d_ff), jnp.bfloat16),
                        pltpu.VMEM((d, 2 * d_ff), jnp.bfloat16),
                        pltpu.VMEM((d_ff, d), jnp.bfloat16),
                        pltpu.VMEM((2, STAGE_ROWS_IN, 2 * d_ff), jnp.float32),
                        pltpu.VMEM((2, STAGE_ROWS_OUT, d), jnp.float32),
                        pltpu.SemaphoreType.DMA((2,)),
                        pltpu.SemaphoreType.DMA((2,))],
        compiler_params=_params(("arbitrary",)),
        name="ffn",
    )(*args)


CONV_ROWS = 128
CONV_LANES = 128


def _dwconv_chunk(u_ref, wdw_ref, bdw_ref, y_ref, t0, l0):
    lanes = pl.ds(l0, CONV_LANES)
    first = CONV_HALO - (CONV_WIDTH - 1)
    acc = jnp.broadcast_to(bdw_ref[:, lanes], (CONV_ROWS, CONV_LANES))
    for r in range(V7X_SUBLANES):
        part = None
        for w in range(CONV_WIDTH):
            off = first + w
            if off % V7X_SUBLANES != r:
                continue
            slab = u_ref[pl.ds(pl.multiple_of(t0 + (off - r), V7X_SUBLANES),
                               CONV_ROWS + (V7X_SUBLANES if r else 0)), lanes]
            term = slab * wdw_ref[pl.ds(w, 1), lanes]
            part = term if part is None else part + term
        acc = acc + part[r:r + CONV_ROWS]
    y_ref[pl.ds(t0, CONV_ROWS), lanes] = acc


def _conv_body(x_ref, g_ref, w1_ref, b1_ref, wdw_ref, bdw_ref, lng_ref, lnb_ref,
               w2_ref, b2_ref, o_ref, u_ref, y_ref, *, ts):
    d = x_ref.shape[-1]

    @pl.when(pl.program_id(1) == 0)
    def _():
        u_ref[0:CONV_HALO, :] = jnp.zeros((CONV_HALO, d), jnp.float32)

    x = x_ref[0]
    h = _rms(x, g_ref[...]).astype(jnp.bfloat16)
    val = _dot(h, w1_ref[:, 0:d]) + b1_ref[:, 0:d]
    gate = _dot(h, w1_ref[:, d:2 * d]) + b1_ref[:, d:2 * d]
    u_ref[CONV_HALO:CONV_HALO + ts, :] = val * jax.nn.sigmoid(gate)

    def chunk(i, carry):
        _dwconv_chunk(u_ref, wdw_ref, bdw_ref, y_ref,
                      pl.multiple_of((i // (d // CONV_LANES)) * CONV_ROWS, CONV_ROWS),
                      pl.multiple_of((i % (d // CONV_LANES)) * CONV_LANES, CONV_LANES))
        return carry

    lax.fori_loop(0, (ts // CONV_ROWS) * (d // CONV_LANES), chunk, 0)
    u_ref[0:CONV_HALO, :] = u_ref[ts:ts + CONV_HALO, :]

    acc = y_ref[...]
    mu = jnp.mean(acc, axis=-1, keepdims=True)
    cen = acc - mu
    var = jnp.mean(cen * cen, axis=-1, keepdims=True)
    y = cen * lax.rsqrt(var + LN_EPS) * lng_ref[...] + lnb_ref[...]
    y = (y * jax.nn.sigmoid(y)).astype(jnp.bfloat16)
    o_ref[0] = x + _dot(y, w2_ref[...]) + b2_ref[...]


def _conv(x, g, w1, b1, wdw, bdw, lng, lnb, w2, b2, *, ts=1024):
    b, s, d = x.shape
    assert s % ts == 0 and ts % CONV_ROWS == 0 and d % CONV_LANES == 0
    assert ts >= CONV_HALO and CONV_WIDTH - 1 <= CONV_HALO and CONV_HALO % V7X_SUBLANES == 0
    row = lambda a: a.reshape(1, -1)
    return pl.pallas_call(
        functools.partial(_conv_body, ts=ts),
        out_shape=jax.ShapeDtypeStruct((b, s, d), jnp.float32),
        grid=(b, s // ts),
        in_specs=[
            pl.BlockSpec((1, ts, d), lambda bi, si: (bi, si, 0)),
            _resident((1, d)),
            _resident((d, 2 * d)),
            _resident((1, 2 * d)),
            _resident((CONV_WIDTH, d)),
            _resident((1, d)),
            _resident((1, d)),
            _resident((1, d)),
            _resident((d, d)),
            _resident((1, d)),
        ],
        out_specs=pl.BlockSpec((1, ts, d), lambda bi, si: (bi, si, 0)),
        scratch_shapes=[pltpu.VMEM((CONV_HALO + ts, d), jnp.float32),
                        pltpu.VMEM((ts, d), jnp.float32)],
        compiler_params=_params(("arbitrary", "arbitrary")),
        name="conv",
    )(x, row(g), w1.astype(jnp.bfloat16), row(b1), wdw, row(bdw), row(lng), row(lnb),
      w2.astype(jnp.bfloat16), row(b2))


def _qkv_body(x_ref, g_ref, w_ref, o_ref, *, d_attn, scale):
    h = _rms(x_ref[...], g_ref[...]).astype(jnp.bfloat16)
    o_ref[:, 0:d_attn] = (_dot(h, w_ref[:, 0:d_attn]) * scale).astype(jnp.bfloat16)
    o_ref[:, d_attn:] = _dot(h, w_ref[:, d_attn:]).astype(jnp.bfloat16)


def _qkv(x, g, w, *, scale, tm=1024):
    t, d = x.shape
    n = w.shape[1]
    assert t % tm == 0
    return pl.pallas_call(
        functools.partial(_qkv_body, d_attn=n // 3, scale=scale),
        out_shape=jax.ShapeDtypeStruct((t, n), jnp.bfloat16),
        grid=(t // tm,),
        in_specs=[pl.BlockSpec((tm, d), lambda i: (i, 0)), _resident((1, d)), _resident((d, n))],
        out_specs=pl.BlockSpec((tm, n), lambda i: (i, 0)),
        compiler_params=_params(("arbitrary",)),
        name="qkv",
    )(x, g.reshape(1, d), w.astype(jnp.bfloat16))


LOG2_E = 1.4426950408889634

DEAD_LOG2 = 160.0

MASKED_SCORE = -1e30

ATTN_BLOCK = 128
ATTN_DEPTH = 3


def _softplus2(y):
    neg_abs = pltpu.bitcast(pltpu.bitcast(y, jnp.uint32) | jnp.uint32(0x80000000), jnp.float32)
    return jnp.maximum(y, 0.0) + jnp.log2(1.0 + jnp.exp2(neg_abs))


def _suffix_sum(sp, tri):
    return _dot(sp.astype(jnp.bfloat16), tri)


def _attn_body(q_ref, k_ref, v_ref, tri_ref, o_ref, c_ref, acc_ref, *, ns, hd):
    g, depth = ATTN_BLOCK, ATTN_DEPTH
    qi = pl.program_id(2)
    base = qi * ns
    lane = lax.broadcasted_iota(jnp.int32, (1, 2 * hd), 1)
    head_lanes = (lane < hd, lane >= hd)
    row = lax.broadcasted_iota(jnp.int32, (g, 2 * g), 0)
    col = lax.broadcasted_iota(jnp.int32, (g, 2 * g), 1)
    causal = (col & (g - 1)) < row
    nt = (((1,), (1,)), ((), ()))

    c_ref[...] = jnp.zeros_like(c_ref)
    acc_ref[...] = jnp.zeros_like(acc_ref)

    def mask_first(z):
        top = jnp.where(causal, z[:g], MASKED_SCORE)
        return top if z.shape[0] == g else jnp.concatenate([top, z[g:]], axis=0)

    def per_head(x):
        return jnp.concatenate([jnp.where(hl, x, jnp.zeros_like(x)) for hl in head_lanes], axis=0)

    def scores(kb, r_lo, r_hi, masked):
        rows = pl.ds(r_lo * g, (r_hi - r_lo + 1) * g)
        k = k_ref[0, pl.ds(pl.multiple_of(kb * g, g), g), :]
        z = lax.dot_general(q_ref[0, rows, :], per_head(k), nt, preferred_element_type=jnp.float32)
        if masked:
            z = mask_first(z)
        sp = _softplus2(z)
        suffix = _suffix_sum(sp, tri_ref[...])
        totals = [suffix[:, h * g:h * g + 1] + sp[:, h * g:h * g + 1] for h in range(2)]
        return z - sp - suffix, totals

    def apply(kb, r_lo, r_hi, w, totals):
        m = (r_hi - r_lo + 1) * g
        rows = pl.ds(r_lo * g, m)
        v = v_ref[0, pl.ds(pl.multiple_of(kb * g, g), g), :]
        c = [c_ref[h, rows, :] for h in range(2)]
        a = jnp.exp2(w - jnp.concatenate([jnp.broadcast_to(ch, (m, g)) for ch in c], axis=1))
        acc_ref[rows, :] += _dot(a.astype(jnp.bfloat16), per_head(v))
        for h in range(2):
            c_ref[h, rows, :] = c[h] + totals[h]

    def step(*blk):
        apply(*blk[:3], *scores(*blk))

    def dead(r_lo):
        return jnp.min(c_ref[:, r_lo * g:, :]) >= DEAD_LOG2

    def band(min_off):
        blks = [(base + off, max(off, 0), min(off + depth - 1, ns - 1), off >= 0)
                for off in range(ns - 1, max(-depth, min_off - 1), -1)]
        partial = [scores(*blk) for blk in blks]
        for blk, (w, totals) in zip(blks, partial):
            apply(*blk[:3], w, totals)

    def ramp(min_off):
        offs = [off for off in range(ns - 1 - depth, -depth, -1) if off >= min_off]
        if offs:
            @pl.when(jnp.logical_not(dead(offs[-1] + depth)))
            def _():
                for off in offs:
                    step(base + off, off + depth, ns - 1, False)

    @pl.when(qi == 0)
    def _():
        band(0)
        ramp(0)

    @pl.when(qi > 0)
    def _():
        band(1 - depth)
        ramp(1 - depth)

        def body(carry):
            kb, _ = carry
            step(kb, 0, ns - 1, False)
            return kb - 1, dead(0).astype(jnp.int32)

        lax.while_loop(lambda carry: (carry[0] >= 0) & (carry[1] == 0), body,
                       (base - depth, dead(0).astype(jnp.int32)))

    o_ref[0] = acc_ref[...].astype(o_ref.dtype)


def _attn(qkv, *, n_heads, tq=2048):
    b, s, n = qkv.shape
    d_attn = n // 3
    hd = d_attn // n_heads
    g = ATTN_BLOCK
    ns = tq // g
    assert 2 * hd == V7X_LANES and s % tq == 0 and tq % g == 0 and ns >= ATTN_DEPTH - 1
    pairs = n_heads // 2
    tri = jnp.tril(jnp.ones((g, g), jnp.bfloat16), -1)
    tri = jnp.kron(jnp.eye(2, dtype=jnp.bfloat16), tri)
    return pl.pallas_call(
        functools.partial(_attn_body, ns=ns, hd=hd),
        out_shape=jax.ShapeDtypeStruct((b, s, d_attn), jnp.bfloat16),
        grid=(b, pairs, s // tq),
        in_specs=[
            pl.BlockSpec((1, tq, 2 * hd), lambda bi, p, i: (bi, i, p)),
            pl.BlockSpec((1, s, 2 * hd), lambda bi, p, i: (bi, 0, pairs + p)),
            pl.BlockSpec((1, s, 2 * hd), lambda bi, p, i: (bi, 0, 2 * pairs + p)),
            _resident((2 * g, 2 * g)),
        ],
        out_specs=pl.BlockSpec((1, tq, 2 * hd), lambda bi, p, i: (bi, i, p)),
        scratch_shapes=[pltpu.VMEM((2, tq, 1), jnp.float32),
                        pltpu.VMEM((tq, 2 * hd), jnp.float32)],
        compiler_params=_params(("arbitrary", "arbitrary", "arbitrary")),
        name="attn",
    )(qkv, qkv, qkv, tri)


def kernel(x, norm_g, final_g, ffn_w_in, ffn_w_out, conv_w_pw1, conv_b_pw1, conv_w_dw, conv_b_dw,
           conv_ln_g, conv_ln_b, conv_w_pw2, conv_b_pw2, attn_w_qkv, attn_w_o):
    b, s, d = x.shape
    depth = norm_g.shape[0]
    x = x.reshape(b * s, d)
    for i in range(depth):
        g = norm_g[i]
        j = i // 2
        x = _ffn(x, g[0], ffn_w_in, ffn_w_out, (i, 0))
        pre = None
        if i % 2 == 0:
            x = _conv(x.reshape(b, s, d), g[1], conv_w_pw1[j], conv_b_pw1[j], conv_w_dw[j],
                      conv_b_dw[j], conv_ln_g[j], conv_ln_b[j], conv_w_pw2[j],
                      conv_b_pw2[j]).reshape(b * s, d)
        else:
            hd = attn_w_qkv.shape[2] // 3 // N_HEADS
            qkv = _qkv(x, g[1], attn_w_qkv[j], scale=hd ** -0.5 * LOG2_E)
            a = _attn(qkv.reshape(b, s, -1), n_heads=N_HEADS)
            pre = (a.reshape(b * s, -1), attn_w_o[j])
        x = _ffn(x, g[2], ffn_w_in, ffn_w_out, (i, 1), pre=pre,
                 final_g=final_g if i == depth - 1 else None)
    return x.reshape(b, s, d)
```

```python
import functools

import jax
import jax.numpy as jnp
from jax import lax
from jax.experimental import pallas as pl
from jax.experimental.pallas import tpu as pltpu

N_HEADS = 16
CONV_WIDTH = 31
RMS_EPS = 1e-6
LN_EPS = 1e-5

V7X_LANES = 128
V7X_SUBLANES = 8
V7X_VMEM_BYTES = 64 * 1024 * 1024
VMEM_LIMIT_BYTES = 56 * 1024 * 1024

CONV_HALO = 32


def _params(semantics):
    return pltpu.CompilerParams(dimension_semantics=semantics,
                                vmem_limit_bytes=VMEM_LIMIT_BYTES)


def _resident(shape):
    zeros = (0,) * len(shape)
    return pl.BlockSpec(shape, lambda *_: zeros, pipeline_mode=pl.Buffered(1))


def _rms(x, g):
    return x * lax.rsqrt(jnp.mean(x * x, axis=-1, keepdims=True) + RMS_EPS) * g


def _dot(a, b):
    return jnp.dot(a, b, preferred_element_type=jnp.float32)


STAGE_ROWS_IN = 64
STAGE_ROWS_OUT = 256


def _stage_weights(src_hbm, dst_ref, stage_ref, sem):
    rows = stage_ref.shape[1]
    n = src_hbm.shape[0] // rows

    def copy(i):
        return pltpu.make_async_copy(src_hbm.at[pl.ds(i * rows, rows), :], stage_ref.at[i % 2],
                                     sem.at[i % 2])

    copy(0).start()
    for i in range(n):
        if i + 1 < n:
            copy(i + 1).start()
        copy(i).wait()
        dst_ref[pl.ds(i * rows, rows), :] = stage_ref[i % 2].astype(dst_ref.dtype)


def _ffn_body(*refs, d_ff, chunk, which, pre_proj, final_norm):
    refs = list(refs)
    x_ref = refs.pop(0)
    a_ref, wa_ref = (refs.pop(0), refs.pop(0)) if pre_proj else (None, None)
    g_ref, win_hbm, wout_hbm = refs.pop(0), refs.pop(0), refs.pop(0)
    fg_ref = refs.pop(0) if final_norm else None
    o_ref, act_ref, win_ref, wout_ref, stage_in, stage_out, sem_in, sem_out = refs

    @pl.when(pl.program_id(0) == 0)
    def _():
        _stage_weights(win_hbm.at[which], win_ref, stage_in, sem_in)
        _stage_weights(wout_hbm.at[which], wout_ref, stage_out, sem_out)

    x = x_ref[...]
    if pre_proj:
        x = x + _dot(a_ref[...], wa_ref[...])
    h = _rms(x, g_ref[...]).astype(jnp.bfloat16)
    for c in range(d_ff // chunk):
        lo = c * chunk
        gate = _dot(h, win_ref[:, lo:lo + chunk])
        up = _dot(h, win_ref[:, d_ff + lo:d_ff + lo + chunk])
        act_ref[:, lo:lo + chunk] = (gate * jax.nn.sigmoid(gate) * up).astype(jnp.bfloat16)
    y = x + 0.5 * _dot(act_ref[...], wout_ref[...])
    if final_norm:
        y = _rms(y, fg_ref[...])
    o_ref[...] = y


def _ffn(x, g, w_in, w_out, which, *, pre=None, final_g=None, tm=1024, chunk=256):
    t, d = x.shape
    d_ff = w_out.shape[-2]
    assert t % tm == 0 and d_ff % chunk == 0
    assert d % STAGE_ROWS_IN == 0 and d_ff % STAGE_ROWS_OUT == 0
    final_norm = final_g is not None
    tile = lambda width: pl.BlockSpec((tm, width), lambda i: (i, 0))
    in_specs, args = [tile(d)], [x]
    if pre is not None:
        a, w_a = pre
        in_specs += [tile(a.shape[1]), _resident(w_a.shape)]
        args += [a, w_a.astype(jnp.bfloat16)]
    hbm = pl.BlockSpec(memory_space=pl.ANY)
    in_specs += [_resident((1, d)), hbm, hbm]
    args += [g.reshape(1, d), w_in, w_out]
    if final_norm:
        in_specs.append(_resident((1, d)))
        args.append(final_g.reshape(1, d))
    return pl.pallas_call(
        functools.partial(_ffn_body, d_ff=d_ff, chunk=chunk, which=which,
                          pre_proj=pre is not None, final_norm=final_norm),
        out_shape=jax.ShapeDtypeStruct((t, d), jnp.float32),
        grid=(t // tm,),
        in_specs=in_specs,
        out_specs=pl.BlockSpec((tm, d), lambda i: (i, 0)),
        scratch_shapes=[pltpu.VMEM((tm, d_ff), jnp.bfloat16),
                        pltpu.VMEM((d, 2 * d_ff), jnp.bfloat16),
                        pltpu.VMEM((d_ff, d), jnp.bfloat16),
                        pltpu.VMEM((2, STAGE_ROWS_IN, 2 * d_ff), jnp.float32),
                        pltpu.VMEM((2, STAGE_ROWS_OUT, d), jnp.float32),
                        pltpu.SemaphoreType.DMA((2,)),
                        pltpu.SemaphoreType.DMA((2,))],
        compiler_params=_params(("arbitrary",)),
        name="ffn",
    )(*args)


CONV_ROWS = 128
CONV_LANES = 128
CONV_PIPE = 256


def _dwconv_chunk(u_ref, wdw_ref, bdw_ref, y_ref, t0, l0):
    lanes = pl.ds(l0, CONV_LANES)
    first = CONV_HALO - (CONV_WIDTH - 1)
    acc = jnp.broadcast_to(bdw_ref[:, lanes], (CONV_ROWS, CONV_LANES))
    for r in range(V7X_SUBLANES):
        part = None
        for w in range(CONV_WIDTH):
            off = first + w
            if off % V7X_SUBLANES != r:
                continue
            slab = u_ref[pl.ds(t0 + (off - r), CONV_ROWS + (V7X_SUBLANES if r else 0)), lanes]
            term = slab * wdw_ref[pl.ds(w, 1), lanes]
            part = term if part is None else part + term
        acc = acc + part[r:r + CONV_ROWS]
    y_ref[pl.ds(t0, CONV_ROWS), lanes] = acc


def _conv_body(x_ref, g_ref, w1_ref, b1_ref, wdw_ref, bdw_ref, lng_ref, lnb_ref,
               w2_ref, b2_ref, o_ref, u_ref, y_ref, *, ts):
    d = x_ref.shape[-1]

    @pl.when(pl.program_id(1) == 0)
    def _():
        u_ref[0:CONV_HALO, :] = jnp.zeros((CONV_HALO, d), jnp.float32)

    def glu(j):
        rows = pl.ds(j * CONV_PIPE, CONV_PIPE)
        h = _rms(x_ref[0, rows, :], g_ref[...]).astype(jnp.bfloat16)
        val = _dot(h, w1_ref[:, 0:d]) + b1_ref[:, 0:d]
        gate = _dot(h, w1_ref[:, d:2 * d]) + b1_ref[:, d:2 * d]
        u_ref[pl.ds(CONV_HALO + j * CONV_PIPE, CONV_PIPE), :] = val * jax.nn.sigmoid(gate)

    def taps(j):
        for t0 in range(j * CONV_PIPE, (j + 1) * CONV_PIPE, CONV_ROWS):
            for l0 in range(0, d, CONV_LANES):
                _dwconv_chunk(u_ref, wdw_ref, bdw_ref, y_ref, t0, l0)

    def out(j):
        rows = pl.ds(j * CONV_PIPE, CONV_PIPE)
        acc = y_ref[rows, :]
        mu = jnp.mean(acc, axis=-1, keepdims=True)
        cen = acc - mu
        var = jnp.mean(cen * cen, axis=-1, keepdims=True)
        y = cen * lax.rsqrt(var + LN_EPS) * lng_ref[...] + lnb_ref[...]
        y = (y * jax.nn.sigmoid(y)).astype(jnp.bfloat16)
        o_ref[0, rows, :] = x_ref[0, rows, :] + _dot(y, w2_ref[...]) + b2_ref[...]

    n = ts // CONV_PIPE
    for s in range(n + 2):
        if s < n:
            glu(s)
        if 1 <= s <= n:
            taps(s - 1)
        if 2 <= s:
            out(s - 2)
    u_ref[0:CONV_HALO, :] = u_ref[ts:ts + CONV_HALO, :]


def _conv(x, g, w1, b1, wdw, bdw, lng, lnb, w2, b2, *, ts=1024):
    b, s, d = x.shape
    assert s % ts == 0 and ts % CONV_PIPE == 0 and CONV_PIPE % CONV_ROWS == 0 and d % CONV_LANES == 0
    assert ts >= CONV_HALO and CONV_WIDTH - 1 <= CONV_HALO and CONV_HALO % V7X_SUBLANES == 0
    row = lambda a: a.reshape(1, -1)
    return pl.pallas_call(
        functools.partial(_conv_body, ts=ts),
        out_shape=jax.ShapeDtypeStruct((b, s, d), jnp.float32),
        grid=(b, s // ts),
        in_specs=[
            pl.BlockSpec((1, ts, d), lambda bi, si: (bi, si, 0)),
            _resident((1, d)),
            _resident((d, 2 * d)),
            _resident((1, 2 * d)),
            _resident((CONV_WIDTH, d)),
            _resident((1, d)),
            _resident((1, d)),
            _resident((1, d)),
            _resident((d, d)),
            _resident((1, d)),
        ],
        out_specs=pl.BlockSpec((1, ts, d), lambda bi, si: (bi, si, 0)),
        scratch_shapes=[pltpu.VMEM((CONV_HALO + ts, d), jnp.float32),
                        pltpu.VMEM((ts, d), jnp.float32)],
        compiler_params=_params(("arbitrary", "arbitrary")),
        name="conv",
    )(x, row(g), w1.astype(jnp.bfloat16), row(b1), wdw, row(bdw), row(lng), row(lnb),
      w2.astype(jnp.bfloat16), row(b2))


def _qkv_body(x_ref, g_ref, w_ref, o_ref, *, d_attn, scale):
    h = _rms(x_ref[...], g_ref[...]).astype(jnp.bfloat16)
    o_ref[:, 0:d_attn] = (_dot(h, w_ref[:, 0:d_attn]) * scale).astype(jnp.bfloat16)
    o_ref[:, d_attn:] = _dot(h, w_ref[:, d_attn:]).astype(jnp.bfloat16)


def _qkv(x, g, w, *, scale, tm=1024):
    t, d = x.shape
    n = w.shape[1]
    assert t % tm == 0
    return pl.pallas_call(
        functools.partial(_qkv_body, d_attn=n // 3, scale=scale),
        out_shape=jax.ShapeDtypeStruct((t, n), jnp.bfloat16),
        grid=(t // tm,),
        in_specs=[pl.BlockSpec((tm, d), lambda i: (i, 0)), _resident((1, d)), _resident((d, n))],
        out_specs=pl.BlockSpec((tm, n), lambda i: (i, 0)),
        compiler_params=_params(("arbitrary",)),
        name="qkv",
    )(x, g.reshape(1, d), w.astype(jnp.bfloat16))


LOG2_E = 1.4426950408889634

DEAD_LOG2 = 160.0

MASKED_SCORE = -1e30

ATTN_BLOCK = 128
ATTN_DEPTH = 3


def _softplus2(y):
    neg_abs = pltpu.bitcast(pltpu.bitcast(y, jnp.uint32) | jnp.uint32(0x80000000), jnp.float32)
    return jnp.maximum(y, 0.0) + jnp.log2(1.0 + jnp.exp2(neg_abs))


def _suffix_sum(sp, tri):
    return _dot(sp.astype(jnp.bfloat16), tri)


def _attn_body(q_ref, k_ref, v_ref, tri_ref, o_ref, c_ref, acc_ref, *, ns, hd):
    g, depth = ATTN_BLOCK, ATTN_DEPTH
    qi = pl.program_id(2)
    base = qi * ns
    lane = lax.broadcasted_iota(jnp.int32, (1, 2 * hd), 1)
    head_lanes = (lane < hd, lane >= hd)
    row = lax.broadcasted_iota(jnp.int32, (g, 2 * g), 0)
    col = lax.broadcasted_iota(jnp.int32, (g, 2 * g), 1)
    causal = (col & (g - 1)) < row
    nt = (((1,), (1,)), ((), ()))

    c_ref[...] = jnp.zeros_like(c_ref)
    acc_ref[...] = jnp.zeros_like(acc_ref)

    def mask_first(z):
        top = jnp.where(causal, z[:g], MASKED_SCORE)
        return top if z.shape[0] == g else jnp.concatenate([top, z[g:]], axis=0)

    def per_head(x):
        return jnp.concatenate([jnp.where(hl, x, jnp.zeros_like(x)) for hl in head_lanes], axis=0)

    def scores(kb, r_lo, r_hi, masked):
        rows = pl.ds(r_lo * g, (r_hi - r_lo + 1) * g)
        k = k_ref[0, pl.ds(pl.multiple_of(kb * g, g), g), :]
        z = lax.dot_general(q_ref[0, rows, :], per_head(k), nt, preferred_element_type=jnp.float32)
        if masked:
            z = mask_first(z)
        sp = _softplus2(z)
        suffix = _suffix_sum(sp, tri_ref[...])
        totals = [suffix[:, h * g:h * g + 1] + sp[:, h * g:h * g + 1] for h in range(2)]
        return z - sp - suffix, totals

    def apply(kb, r_lo, r_hi, w, totals):
        m = (r_hi - r_lo + 1) * g
        rows = pl.ds(r_lo * g, m)
        v = v_ref[0, pl.ds(pl.multiple_of(kb * g, g), g), :]
        c = [c_ref[h, rows, :] for h in range(2)]
        a = jnp.exp2(w - jnp.concatenate([jnp.broadcast_to(ch, (m, g)) for ch in c], axis=1))
        acc_ref[rows, :] += _dot(a.astype(jnp.bfloat16), per_head(v))
        for h in range(2):
            c_ref[h, rows, :] = c[h] + totals[h]

    def step(*blk):
        apply(*blk[:3], *scores(*blk))

    def dead(r_lo):
        return jnp.min(c_ref[:, r_lo * g:, :]) >= DEAD_LOG2

    def band(min_off):
        blks = [(base + off, max(off, 0), min(off + depth - 1, ns - 1), off >= 0)
                for off in range(ns - 1, max(-depth, min_off - 1), -1)]
        partial = [scores(*blk) for blk in blks]
        for blk, (w, totals) in zip(blks, partial):
            apply(*blk[:3], w, totals)

    def ramp(min_off):
        offs = [off for off in range(ns - 1 - depth, -depth, -1) if off >= min_off]
        if offs:
            @pl.when(jnp.logical_not(dead(offs[-1] + depth)))
            def _():
                for off in offs:
                    step(base + off, off + depth, ns - 1, False)

    @pl.when(qi == 0)
    def _():
        band(0)
        ramp(0)

    @pl.when(qi > 0)
    def _():
        band(1 - depth)
        ramp(1 - depth)

        def body(carry):
            kb, _ = carry
            step(kb, 0, ns - 1, False)
            return kb - 1, dead(0).astype(jnp.int32)

        lax.while_loop(lambda carry: (carry[0] >= 0) & (carry[1] == 0), body,
                       (base - depth, dead(0).astype(jnp.int32)))

    o_ref[0] = acc_ref[...].astype(o_ref.dtype)


def _attn(qkv, *, n_heads, tq=2048):
    b, s, n = qkv.shape
    d_attn = n // 3
    hd = d_attn // n_heads
    g = ATTN_BLOCK
    ns = tq // g
    assert 2 * hd == V7X_LANES and s % tq == 0 and tq % g == 0 and ns >= ATTN_DEPTH - 1
    pairs = n_heads // 2
    tri = jnp.tril(jnp.ones((g, g), jnp.bfloat16), -1)
    tri = jnp.kron(jnp.eye(2, dtype=jnp.bfloat16), tri)
    return pl.pallas_call(
        functools.partial(_attn_body, ns=ns, hd=hd),
        out_shape=jax.ShapeDtypeStruct((b, s, d_attn), jnp.bfloat16),
        grid=(b, pairs, s // tq),
        in_specs=[
            pl.BlockSpec((1, tq, 2 * hd), lambda bi, p, i: (bi, i, p)),
            pl.BlockSpec((1, s, 2 * hd), lambda bi, p, i: (bi, 0, pairs + p)),
            pl.BlockSpec((1, s, 2 * hd), lambda bi, p, i: (bi, 0, 2 * pairs + p)),
            _resident((2 * g, 2 * g)),
        ],
        out_specs=pl.BlockSpec((1, tq, 2 * hd), lambda bi, p, i: (bi, i, p)),
        scratch_shapes=[pltpu.VMEM((2, tq, 1), jnp.float32),
                        pltpu.VMEM((tq, 2 * hd), jnp.float32)],
        compiler_params=_params(("arbitrary", "arbitrary", "arbitrary")),
        name="attn",
    )(qkv, qkv, qkv, tri)


def kernel(x, norm_g, final_g, ffn_w_in, ffn_w_out, conv_w_pw1, conv_b_pw1, conv_w_dw, conv_b_dw,
           conv_ln_g, conv_ln_b, conv_w_pw2, conv_b_pw2, attn_w_qkv, attn_w_o):
    b, s, d = x.shape
    depth = norm_g.shape[0]
    x = x.reshape(b * s, d)
    for i in range(depth):
        g = norm_g[i]
        j = i // 2
        x = _ffn(x, g[0], ffn_w_in, ffn_w_out, (i, 0))
        pre = None
        if i % 2 == 0:
            x = _conv(x.reshape(b, s, d), g[1], conv_w_pw1[j], conv_b_pw1[j], conv_w_dw[j],
                      conv_b_dw[j], conv_ln_g[j], conv_ln_b[j], conv_w_pw2[j],
                      conv_b_pw2[j]).reshape(b * s, d)
        else:
            hd = attn_w_qkv.shape[2] // 3 // N_HEADS
            qkv = _qkv(x, g[1], attn_w_qkv[j], scale=hd ** -0.5 * LOG2_E)
            a = _attn(qkv.reshape(b, s, -1), n_heads=N_HEADS)
            pre = (a.reshape(b * s, -1), attn_w_o[j])
        x = _ffn(x, g[2], ffn_w_in, ffn_w_out, (i, 1), pre=pre,
                 final_g=final_g if i == depth - 1 else None)
    return x.reshape(b, s, d)
```

```python
import functools

import jax
import jax.numpy as jnp
from jax import lax
from jax.experimental import pallas as pl
from jax.experimental.pallas import tpu as pltpu

N_HEADS = 16
CONV_WIDTH = 31
RMS_EPS = 1e-6
LN_EPS = 1e-5

V7X_LANES = 128
V7X_SUBLANES = 8
V7X_VMEM_BYTES = 64 * 1024 * 1024
VMEM_LIMIT_BYTES = 56 * 1024 * 1024

CONV_HALO = 32


def _params(semantics):
    return pltpu.CompilerParams(dimension_semantics=semantics,
                                vmem_limit_bytes=VMEM_LIMIT_BYTES)


def _resident(shape):
    zeros = (0,) * len(shape)
    return pl.BlockSpec(shape, lambda *_: zeros, pipeline_mode=pl.Buffered(1))


def _rms(x, g):
    return x * lax.rsqrt(jnp.mean(x * x, axis=-1, keepdims=True) + RMS_EPS) * g


def _dot(a, b):
    return jnp.dot(a, b, preferred_element_type=jnp.float32)


STAGE_ROWS_IN = 64
STAGE_ROWS_OUT = 256


def _stage_weights(src_hbm, dst_ref, stage_ref, sem):
    rows = stage_ref.shape[1]
    n = src_hbm.shape[0] // rows

    def copy(i):
        return pltpu.make_async_copy(src_hbm.at[pl.ds(i * rows, rows), :], stage_ref.at[i % 2],
                                     sem.at[i % 2])

    copy(0).start()
    for i in range(n):
        if i + 1 < n:
            copy(i + 1).start()
        copy(i).wait()
        dst_ref[pl.ds(i * rows, rows), :] = stage_ref[i % 2].astype(dst_ref.dtype)


def _ffn_body(*refs, d_ff, chunk, which, pre_proj, final_norm):
    refs = list(refs)
    x_ref = refs.pop(0)
    a_ref, wa_ref = (refs.pop(0), refs.pop(0)) if pre_proj else (None, None)
    g_ref, win_hbm, wout_hbm = refs.pop(0), refs.pop(0), refs.pop(0)
    fg_ref = refs.pop(0) if final_norm else None
    o_ref, act_ref, win_ref, wout_ref, stage_in, stage_out, sem_in, sem_out = refs

    @pl.when(pl.program_id(0) == 0)
    def _():
        _stage_weights(win_hbm.at[which], win_ref, stage_in, sem_in)
        _stage_weights(wout_hbm.at[which], wout_ref, stage_out, sem_out)

    x = x_ref[...]
    if pre_proj:
        x = x + _dot(a_ref[...], wa_ref[...])
    h = _rms(x, g_ref[...]).astype(jnp.bfloat16)
    for c in range(d_ff // chunk):
        lo = c * chunk
        gate = _dot(h, win_ref[:, lo:lo + chunk])
        up = _dot(h, win_ref[:, d_ff + lo:d_ff + lo + chunk])
        act_ref[:, lo:lo + chunk] = (gate * jax.nn.sigmoid(gate) * up).astype(jnp.bfloat16)
    y = x + 0.5 * _dot(act_ref[...], wout_ref[...])
    if final_norm:
        y = _rms(y, fg_ref[...])
    o_ref[...] = y


def _ffn(x, g, w_in, w_out, which, *, pre=None, final_g=None, tm=1024, chunk=256):
    t, d = x.shape
    d_ff = w_out.shape[-2]
    assert t % tm == 0 and d_ff % chunk == 0
    assert d % STAGE_ROWS_IN == 0 and d_ff % STAGE_ROWS_OUT == 0
    final_norm = final_g is not None
    tile = lambda width: pl.BlockSpec((tm, width), lambda i: (i, 0))
    in_specs, args = [tile(d)], [x]
    if pre is not None:
        a, w_a = pre
        in_specs += [tile(a.shape[1]), _resident(w_a.shape)]
        args += [a, w_a.astype(jnp.bfloat16)]
    hbm = pl.BlockSpec(memory_space=pl.ANY)
    in_specs += [_resident((1, d)), hbm, hbm]
    args += [g.reshape(1, d), w_in, w_out]
    if final_norm:
        in_specs.append(_resident((1, d)))
        args.append(final_g.reshape(1, d))
    return pl.pallas_call(
        functools.partial(_ffn_body, d_ff=d_ff, chunk=chunk, which=which,
                          pre_proj=pre is not None, final_norm=final_norm),
        out_shape=jax.ShapeDtypeStruct((t, d), jnp.float32),
        grid=(t // tm,),
        in_specs=in_specs,
        out_specs=pl.BlockSpec((tm, d), lambda i: (i, 0)),
        scratch_shapes=[pltpu.VMEM((tm, d_ff), jnp.bfloat16),
                        pltpu.VMEM((d, 2 * d_ff), jnp.bfloat16),
                        pltpu.VMEM((d_ff, d), jnp.bfloat16),
                        pltpu.VMEM((2, STAGE_ROWS_IN, 2 * d_ff), jnp.float32),
                        pltpu.VMEM((2, STAGE_ROWS_OUT, d), jnp.float32),
                        pltpu.SemaphoreType.DMA((2,)),
                        pltpu.SemaphoreType.DMA((2,))],
        compiler_params=_params(("arbitrary",)),
        name="ffn",
    )(*args)


CONV_ROWS = 128
CONV_LANES = 128
CONV_PIPE = 256


def _dwconv_chunk(u_ref, wdw_ref, bdw_ref, y_ref, t0, l0):
    lanes = pl.ds(l0, CONV_LANES)
    first = CONV_HALO - (CONV_WIDTH - 1)
    acc = jnp.broadcast_to(bdw_ref[:, lanes], (CONV_ROWS, CONV_LANES))
    for r in range(V7X_SUBLANES):
        part = None
        for w in range(CONV_WIDTH):
            off = first + w
            if off % V7X_SUBLANES != r:
                continue
            slab = u_ref[pl.ds(t0 + (off - r), CONV_ROWS + (V7X_SUBLANES if r else 0)), lanes]
            term = slab * wdw_ref[pl.ds(w, 1), lanes]
            part = term if part is None else part + term
        acc = acc + part[r:r + CONV_ROWS]
    y_ref[pl.ds(t0, CONV_ROWS), lanes] = acc


def _conv_body(x_ref, g_ref, w1_ref, b1_ref, wdw_ref, bdw_ref, lng_ref, lnb_ref,
               w2_ref, b2_ref, o_ref, u_ref, y_ref, *, ts):
    d = x_ref.shape[-1]

    @pl.when(pl.program_id(1) == 0)
    def _():
        u_ref[0:CONV_HALO, :] = jnp.zeros((CONV_HALO, d), jnp.float32)

    def glu(j):
        rows = pl.ds(j * CONV_PIPE, CONV_PIPE)
        h = _rms(x_ref[0, rows, :], g_ref[...]).astype(jnp.bfloat16)
        val = _dot(h, w1_ref[:, 0:d]) + b1_ref[:, 0:d]
        gate = _dot(h, w1_ref[:, d:2 * d]) + b1_ref[:, d:2 * d]
        u_ref[pl.ds(CONV_HALO + j * CONV_PIPE, CONV_PIPE), :] = val * jax.nn.sigmoid(gate)

    def taps(j):
        for t0 in range(j * CONV_PIPE, (j + 1) * CONV_PIPE, CONV_ROWS):
            for l0 in range(0, d, CONV_LANES):
                _dwconv_chunk(u_ref, wdw_ref, bdw_ref, y_ref, t0, l0)

    def out(j):
        rows = pl.ds(j * CONV_PIPE, CONV_PIPE)
        acc = y_ref[rows, :]
        mu = jnp.mean(acc, axis=-1, keepdims=True)
        cen = acc - mu
        var = jnp.mean(cen * cen, axis=-1, keepdims=True)
        y = cen * lax.rsqrt(var + LN_EPS) * lng_ref[...] + lnb_ref[...]
        y = (y * jax.nn.sigmoid(y)).astype(jnp.bfloat16)
        o_ref[0, rows, :] = x_ref[0, rows, :] + _dot(y, w2_ref[...]) + b2_ref[...]

    n = ts // CONV_PIPE
    for s in range(n + 2):
        if s < n:
            glu(s)
        if 1 <= s <= n:
            taps(s - 1)
        if 2 <= s:
            out(s - 2)
    u_ref[0:CONV_HALO, :] = u_ref[ts:ts + CONV_HALO, :]


def _conv(x, g, w1, b1, wdw, bdw, lng, lnb, w2, b2, *, ts=1024):
    b, s, d = x.shape
    assert s % ts == 0 and ts % CONV_PIPE == 0 and CONV_PIPE % CONV_ROWS == 0 and d % CONV_LANES == 0
    assert ts >= CONV_HALO and CONV_WIDTH - 1 <= CONV_HALO and CONV_HALO % V7X_SUBLANES == 0
    row = lambda a: a.reshape(1, -1)
    return pl.pallas_call(
        functools.partial(_conv_body, ts=ts),
        out_shape=jax.ShapeDtypeStruct((b, s, d), jnp.float32),
        grid=(b, s // ts),
        in_specs=[
            pl.BlockSpec((1, ts, d), lambda bi, si: (bi, si, 0)),
            _resident((1, d)),
            _resident((d, 2 * d)),
            _resident((1, 2 * d)),
            _resident((CONV_WIDTH, d)),
            _resident((1, d)),
            _resident((1, d)),
            _resident((1, d)),
            _resident((d, d)),
            _resident((1, d)),
        ],
        out_specs=pl.BlockSpec((1, ts, d), lambda bi, si: (bi, si, 0)),
        scratch_shapes=[pltpu.VMEM((CONV_HALO + ts, d), jnp.float32),
                        pltpu.VMEM((ts, d), jnp.float32)],
        compiler_params=_params(("arbitrary", "arbitrary")),
        name="conv",
    )(x, row(g), w1.astype(jnp.bfloat16), row(b1), wdw, row(bdw), row(lng), row(lnb),
      w2.astype(jnp.bfloat16), row(b2))


def _qkv_body(x_ref, g_ref, w_ref, o_ref, *, d_attn, scale):
    h = _rms(x_ref[...], g_ref[...]).astype(jnp.bfloat16)
    o_ref[:, 0:d_attn] = (_dot(h, w_ref[:, 0:d_attn]) * scale).astype(jnp.bfloat16)
    o_ref[:, d_attn:] = _dot(h, w_ref[:, d_attn:]).astype(jnp.bfloat16)


def _qkv(x, g, w, *, scale, tm=1024):
    t, d = x.shape
    n = w.shape[1]
    assert t % tm == 0
    return pl.pallas_call(
        functools.partial(_qkv_body, d_attn=n // 3, scale=scale),
        out_shape=jax.ShapeDtypeStruct((t, n), jnp.bfloat16),
        grid=(t // tm,),
        in_specs=[pl.BlockSpec((tm, d), lambda i: (i, 0)), _resident((1, d)), _resident((d, n))],
        out_specs=pl.BlockSpec((tm, n), lambda i: (i, 0)),
        compiler_params=_params(("arbitrary",)),
        name="qkv",
    )(x, g.reshape(1, d), w.astype(jnp.bfloat16))


LOG2_E = 1.4426950408889634

DEAD_LOG2 = 160.0

MASKED_SCORE = -1e30

ATTN_BLOCK = 128
ATTN_DEPTH = 3


def _softplus2(y):
    neg_abs = pltpu.bitcast(pltpu.bitcast(y, jnp.uint32) | jnp.uint32(0x80000000), jnp.float32)
    return jnp.maximum(y, 0.0) + jnp.log2(1.0 + jnp.exp2(neg_abs))


def _suffix_sum(sp, tri):
    return _dot(sp.astype(jnp.bfloat16), tri)


def _attn_body(q_ref, k_ref, v_ref, tri_ref, o_ref, c_ref, acc_ref, *, ns, hd):
    g, depth = ATTN_BLOCK, ATTN_DEPTH
    qi = pl.program_id(2)
    base = qi * ns
    lane = lax.broadcasted_iota(jnp.int32, (1, 2 * hd), 1)
    head_lanes = (lane < hd, lane >= hd)
    row = lax.broadcasted_iota(jnp.int32, (g, 2 * g), 0)
    col = lax.broadcasted_iota(jnp.int32, (g, 2 * g), 1)
    causal = (col & (g - 1)) < row
    nt = (((1,), (1,)), ((), ()))

    c_ref[...] = jnp.zeros_like(c_ref)
    acc_ref[...] = jnp.zeros_like(acc_ref)

    def mask_first(z):
        top = jnp.where(causal, z[:g], MASKED_SCORE)
        return top if z.shape[0] == g else jnp.concatenate([top, z[g:]], axis=0)

    def per_head(x):
        return jnp.concatenate([jnp.where(hl, x, jnp.zeros_like(x)) for hl in head_lanes], axis=0)

    def scores(kb, r_lo, r_hi, masked):
        rows = pl.ds(r_lo * g, (r_hi - r_lo + 1) * g)
        k = k_ref[0, pl.ds(pl.multiple_of(kb * g, g), g), :]
        z = lax.dot_general(q_ref[0, rows, :], per_head(k), nt, preferred_element_type=jnp.float32)
        if masked:
            z = mask_first(z)
        sp = _softplus2(z)
        suffix = _suffix_sum(sp, tri_ref[...])
        totals = [suffix[:, h * g:h * g + 1] + sp[:, h * g:h * g + 1] for h in range(2)]
        return z - sp - suffix, totals

    def apply(kb, r_lo, r_hi, w, totals):
        m = (r_hi - r_lo + 1) * g
        rows = pl.ds(r_lo * g, m)
        v = v_ref[0, pl.ds(pl.multiple_of(kb * g, g), g), :]
        c = [c_ref[h, rows, :] for h in range(2)]
        a = jnp.exp2(w - jnp.concatenate([jnp.broadcast_to(ch, (m, g)) for ch in c], axis=1))
        acc_ref[rows, :] += _dot(a.astype(jnp.bfloat16), per_head(v))
        for h in range(2):
            c_ref[h, rows, :] = c[h] + totals[h]

    def step(*blk):
        apply(*blk[:3], *scores(*blk))

    def dead(r_lo):
        return jnp.min(c_ref[:, r_lo * g:, :]) >= DEAD_LOG2

    def band(min_off):
        blks = [(base + off, max(off, 0), min(off + depth - 1, ns - 1), off >= 0)
                for off in range(ns - 1, max(-depth, min_off - 1), -1)]
        partial = [scores(*blk) for blk in blks]
        for blk, (w, totals) in zip(blks, partial):
            apply(*blk[:3], w, totals)

    def ramp(min_off, rows_dead=None):
        offs = [off for off in range(ns - 1 - depth, -depth, -1) if off >= min_off]
        if offs:
            if rows_dead is None:
                rows_dead = dead(offs[-1] + depth)

            @pl.when(jnp.logical_not(rows_dead))
            def _():
                for off in offs:
                    step(base + off, off + depth, ns - 1, False)

    @pl.when(qi == 0)
    def _():
        band(0)
        ramp(0)

    @pl.when(qi > 0)
    def _():
        band(1 - depth)
        tile_dead = dead(0)
        ramp(1 - depth, tile_dead)

        def body(carry):
            kb, _ = carry
            step(kb, 0, ns - 1, False)
            return kb - 1, dead(0).astype(jnp.int32)

        lax.while_loop(lambda carry: (carry[0] >= 0) & (carry[1] == 0), body,
                       (base - depth, tile_dead.astype(jnp.int32)))

    o_ref[0] = acc_ref[...].astype(o_ref.dtype)


def _attn(qkv, *, n_heads, tq=2048):
    b, s, n = qkv.shape
    d_attn = n // 3
    hd = d_attn // n_heads
    g = ATTN_BLOCK
    ns = tq // g
    assert 2 * hd == V7X_LANES and s % tq == 0 and tq % g == 0 and ns >= ATTN_DEPTH - 1
    pairs = n_heads // 2
    tri = jnp.tril(jnp.ones((g, g), jnp.bfloat16), -1)
    tri = jnp.kron(jnp.eye(2, dtype=jnp.bfloat16), tri)
    return pl.pallas_call(
        functools.partial(_attn_body, ns=ns, hd=hd),
        out_shape=jax.ShapeDtypeStruct((b, s, d_attn), jnp.bfloat16),
        grid=(b, pairs, s // tq),
        in_specs=[
            pl.BlockSpec((1, tq, 2 * hd), lambda bi, p, i: (bi, i, p)),
            pl.BlockSpec((1, s, 2 * hd), lambda bi, p, i: (bi, 0, pairs + p)),
            pl.BlockSpec((1, s, 2 * hd), lambda bi, p, i: (bi, 0, 2 * pairs + p)),
            _resident((2 * g, 2 * g)),
        ],
        out_specs=pl.BlockSpec((1, tq, 2 * hd), lambda bi, p, i: (bi, i, p)),
        scratch_shapes=[pltpu.VMEM((2, tq, 1), jnp.float32),
                        pltpu.VMEM((tq, 2 * hd), jnp.float32)],
        compiler_params=_params(("arbitrary", "arbitrary", "arbitrary")),
        name="attn",
    )(qkv, qkv, qkv, tri)


def kernel(x, norm_g, final_g, ffn_w_in, ffn_w_out, conv_w_pw1, conv_b_pw1, conv_w_dw, conv_b_dw,
           conv_ln_g, conv_ln_b, conv_w_pw2, conv_b_pw2, attn_w_qkv, attn_w_o):
    b, s, d = x.shape
    depth = norm_g.shape[0]
    x = x.reshape(b * s, d)
    for i in range(depth):
        g = norm_g[i]
        j = i // 2
        x = _ffn(x, g[0], ffn_w_in, ffn_w_out, (i, 0))
        pre = None
        if i % 2 == 0:
            x = _conv(x.reshape(b, s, d), g[1], conv_w_pw1[j], conv_b_pw1[j], conv_w_dw[j],
                      conv_b_dw[j], conv_ln_g[j], conv_ln_b[j], conv_w_pw2[j],
                      conv_b_pw2[j]).reshape(b * s, d)
        else:
            hd = attn_w_qkv.shape[2] // 3 // N_HEADS
            qkv = _qkv(x, g[1], attn_w_qkv[j], scale=hd ** -0.5 * LOG2_E)
            a = _attn(qkv.reshape(b, s, -1), n_heads=N_HEADS)
            pre = (a.reshape(b * s, -1), attn_w_o[j])
        x = _ffn(x, g[2], ffn_w_in, ffn_w_out, (i, 1), pre=pre,
                 final_g=final_g if i == depth - 1 else None)
    return x.reshape(b, s, d)
```

```python
import functools

import jax
import jax.numpy as jnp
from jax import lax
from jax.experimental import pallas as pl
from jax.experimental.pallas import tpu as pltpu

N_HEADS = 16
CONV_WIDTH = 31
RMS_EPS = 1e-6
LN_EPS = 1e-5

V7X_LANES = 128
V7X_SUBLANES = 8
V7X_VMEM_BYTES = 64 * 1024 * 1024
VMEM_LIMIT_BYTES = 56 * 1024 * 1024

CONV_HALO = 32


def _params(semantics):
    return pltpu.CompilerParams(dimension_semantics=semantics,
                                vmem_limit_bytes=VMEM_LIMIT_BYTES)


def _resident(shape):
    zeros = (0,) * len(shape)
    return pl.BlockSpec(shape, lambda *_: zeros, pipeline_mode=pl.Buffered(1))


def _rms(x, g):
    return x * lax.rsqrt(jnp.mean(x * x, axis=-1, keepdims=True) + RMS_EPS) * g


def _dot(a, b):
    return jnp.dot(a, b, preferred_element_type=jnp.float32)


STAGE_ROWS_IN = 64
STAGE_ROWS_OUT = 256


def _stage_weights(src_hbm, dst_ref, stage_ref, sem):
    rows = stage_ref.shape[1]
    n = src_hbm.shape[0] // rows

    def copy(i):
        return pltpu.make_async_copy(src_hbm.at[pl.ds(i * rows, rows), :], stage_ref.at[i % 2],
                                     sem.at[i % 2])

    copy(0).start()
    for i in range(n):
        if i + 1 < n:
            copy(i + 1).start()
        copy(i).wait()
        dst_ref[pl.ds(i * rows, rows), :] = stage_ref[i % 2].astype(dst_ref.dtype)


def _ffn_body(*refs, d_ff, chunk, which, pre_proj, final_norm):
    refs = list(refs)
    x_ref = refs.pop(0)
    a_ref, wa_ref = (refs.pop(0), refs.pop(0)) if pre_proj else (None, None)
    g_ref, win_hbm, wout_hbm = refs.pop(0), refs.pop(0), refs.pop(0)
    fg_ref = refs.pop(0) if final_norm else None
    o_ref, act_ref, win_ref, wout_ref, stage_in, stage_out, sem_in, sem_out = refs

    @pl.when(pl.program_id(0) == 0)
    def _():
        _stage_weights(win_hbm.at[which], win_ref, stage_in, sem_in)
        _stage_weights(wout_hbm.at[which], wout_ref, stage_out, sem_out)

    x = x_ref[...]
    if pre_proj:
        x = x + _dot(a_ref[...], wa_ref[...])
    h = _rms(x, g_ref[...]).astype(jnp.bfloat16)
    for c in range(d_ff // chunk):
        lo = c * chunk
        gate = _dot(h, win_ref[:, lo:lo + chunk])
        up = _dot(h, win_ref[:, d_ff + lo:d_ff + lo + chunk])
        act_ref[:, lo:lo + chunk] = (gate * jax.nn.sigmoid(gate) * up).astype(jnp.bfloat16)
    y = x + 0.5 * _dot(act_ref[...], wout_ref[...])
    if final_norm:
        y = _rms(y, fg_ref[...])
    o_ref[...] = y


def _ffn(x, g, w_in, w_out, which, *, pre=None, final_g=None, tm=1024, chunk=256):
    t, d = x.shape
    d_ff = w_out.shape[-2]
    assert t % tm == 0 and d_ff % chunk == 0
    assert d % STAGE_ROWS_IN == 0 and d_ff % STAGE_ROWS_OUT == 0
    final_norm = final_g is not None
    tile = lambda width: pl.BlockSpec((tm, width), lambda i: (i, 0))
    in_specs, args = [tile(d)], [x]
    if pre is not None:
        a, w_a = pre
        in_specs += [tile(a.shape[1]), _resident(w_a.shape)]
        args += [a, w_a.astype(jnp.bfloat16)]
    hbm = pl.BlockSpec(memory_space=pl.ANY)
    in_specs += [_resident((1, d)), hbm, hbm]
    args += [g.reshape(1, d), w_in, w_out]
    if final_norm:
        in_specs.append(_resident((1, d)))
        args.append(final_g.reshape(1, d))
    return pl.pallas_call(
        functools.partial(_ffn_body, d_ff=d_ff, chunk=chunk, which=which,
                          pre_proj=pre is not None, final_norm=final_norm),
        out_shape=jax.ShapeDtypeStruct((t, d), jnp.float32),
        grid=(t // tm,),
        in_specs=in_specs,
        out_specs=pl.BlockSpec((tm, d), lambda i: (i, 0)),
        scratch_shapes=[pltpu.VMEM((tm, d_ff), jnp.bfloat16),
                        pltpu.VMEM((d, 2 * d_ff), jnp.bfloat16),
                        pltpu.VMEM((d_ff, d), jnp.bfloat16),
                        pltpu.VMEM((2, STAGE_ROWS_IN, 2 * d_ff), jnp.float32),
                        pltpu.VMEM((2, STAGE_ROWS_OUT, d), jnp.float32),
                        pltpu.SemaphoreType.DMA((2,)),
                        pltpu.SemaphoreType.DMA((2,))],
        compiler_params=_params(("arbitrary",)),
        name="ffn",
    )(*args)


CONV_ROWS = 128
CONV_LANES = 128
CONV_PIPE = 256


def _dwconv_chunk(u_ref, wdw_ref, bdw_ref, y_ref, t0, l0):
    lanes = pl.ds(l0, CONV_LANES)
    first = CONV_HALO - (CONV_WIDTH - 1)
    acc = jnp.broadcast_to(bdw_ref[:, lanes], (CONV_ROWS, CONV_LANES))
    for r in range(V7X_SUBLANES):
        part = None
        for w in range(CONV_WIDTH):
            off = first + w
            if off % V7X_SUBLANES != r:
                continue
            slab = u_ref[pl.ds(t0 + (off - r), CONV_ROWS + (V7X_SUBLANES if r else 0)), lanes]
            term = slab * wdw_ref[pl.ds(w, 1), lanes]
            part = term if part is None else part + term
        acc = acc + part[r:r + CONV_ROWS]
    y_ref[pl.ds(t0, CONV_ROWS), lanes] = acc


def _conv_body(x_ref, g_ref, w1_ref, b1_ref, wdw_ref, bdw_ref, lng_ref, lnb_ref,
               w2_ref, b2_ref, o_ref, u_ref, y_ref, *, ts):
    d = x_ref.shape[-1]

    @pl.when(pl.program_id(1) == 0)
    def _():
        u_ref[0:CONV_HALO, :] = jnp.zeros((CONV_HALO, d), jnp.float32)

    def glu(j):
        rows = pl.ds(j * CONV_PIPE, CONV_PIPE)
        h = _rms(x_ref[0, rows, :], g_ref[...]).astype(jnp.bfloat16)
        val = _dot(h, w1_ref[:, 0:d]) + b1_ref[:, 0:d]
        gate = _dot(h, w1_ref[:, d:2 * d]) + b1_ref[:, d:2 * d]
        u_ref[pl.ds(CONV_HALO + j * CONV_PIPE, CONV_PIPE), :] = val * jax.nn.sigmoid(gate)

    def taps(j):
        for t0 in range(j * CONV_PIPE, (j + 1) * CONV_PIPE, CONV_ROWS):
            for l0 in range(0, d, CONV_LANES):
                _dwconv_chunk(u_ref, wdw_ref, bdw_ref, y_ref, t0, l0)

    def out(j):
        rows = pl.ds(j * CONV_PIPE, CONV_PIPE)
        acc = y_ref[rows, :]
        mu = jnp.mean(acc, axis=-1, keepdims=True)
        cen = acc - mu
        var = jnp.mean(cen * cen, axis=-1, keepdims=True)
        y = cen * lax.rsqrt(var + LN_EPS) * lng_ref[...] + lnb_ref[...]
        y = (y * jax.nn.sigmoid(y)).astype(jnp.bfloat16)
        o_ref[0, rows, :] = x_ref[0, rows, :] + _dot(y, w2_ref[...]) + b2_ref[...]

    n = ts // CONV_PIPE
    for s in range(n + 2):
        if s < n:
            glu(s)
        if 1 <= s <= n:
            taps(s - 1)
        if 2 <= s:
            out(s - 2)
    u_ref[0:CONV_HALO, :] = u_ref[ts:ts + CONV_HALO, :]


def _conv(x, g, w1, b1, wdw, bdw, lng, lnb, w2, b2, *, ts=1024):
    b, s, d = x.shape
    assert s % ts == 0 and ts % CONV_PIPE == 0 and CONV_PIPE % CONV_ROWS == 0 and d % CONV_LANES == 0
    assert ts >= CONV_HALO and CONV_WIDTH - 1 <= CONV_HALO and CONV_HALO % V7X_SUBLANES == 0
    row = lambda a: a.reshape(1, -1)
    return pl.pallas_call(
        functools.partial(_conv_body, ts=ts),
        out_shape=jax.ShapeDtypeStruct((b, s, d), jnp.float32),
        grid=(b, s // ts),
        in_specs=[
            pl.BlockSpec((1, ts, d), lambda bi, si: (bi, si, 0)),
            _resident((1, d)),
            _resident((d, 2 * d)),
            _resident((1, 2 * d)),
            _resident((CONV_WIDTH, d)),
            _resident((1, d)),
            _resident((1, d)),
            _resident((1, d)),
            _resident((d, d)),
            _resident((1, d)),
        ],
        out_specs=pl.BlockSpec((1, ts, d), lambda bi, si: (bi, si, 0)),
        scratch_shapes=[pltpu.VMEM((CONV_HALO + ts, d), jnp.float32),
                        pltpu.VMEM((ts, d), jnp.float32)],
        compiler_params=_params(("arbitrary", "arbitrary")),
        name="conv",
    )(x, row(g), w1.astype(jnp.bfloat16), row(b1), wdw, row(bdw), row(lng), row(lnb),
      w2.astype(jnp.bfloat16), row(b2))


def _qkv_body(x_ref, g_ref, w_ref, o_ref, *, d_attn, scale):
    h = _rms(x_ref[...], g_ref[...]).astype(jnp.bfloat16)
    o_ref[:, 0:d_attn] = (_dot(h, w_ref[:, 0:d_attn]) * scale).astype(jnp.bfloat16)
    o_ref[:, d_attn:] = _dot(h, w_ref[:, d_attn:]).astype(jnp.bfloat16)


def _qkv(x, g, w, *, scale, tm=1024):
    t, d = x.shape
    n = w.shape[1]
    assert t % tm == 0
    return pl.pallas_call(
        functools.partial(_qkv_body, d_attn=n // 3, scale=scale),
        out_shape=jax.ShapeDtypeStruct((t, n), jnp.bfloat16),
        grid=(t // tm,),
        in_specs=[pl.BlockSpec((tm, d), lambda i: (i, 0)), _resident((1, d)), _resident((d, n))],
        out_specs=pl.BlockSpec((tm, n), lambda i: (i, 0)),
        compiler_params=_params(("arbitrary",)),
        name="qkv",
    )(x, g.reshape(1, d), w.astype(jnp.bfloat16))


LOG2_E = 1.4426950408889634

DEAD_LOG2 = 160.0

MASKED_SCORE = -1e30

ATTN_BLOCK = 128
ATTN_DEPTH = 3


def _softplus2(y):
    neg_abs = pltpu.bitcast(pltpu.bitcast(y, jnp.uint32) | jnp.uint32(0x80000000), jnp.float32)
    return jnp.maximum(y, 0.0) + jnp.log2(1.0 + jnp.exp2(neg_abs))


def _suffix_sum(sp, tri):
    return _dot(sp.astype(jnp.bfloat16), tri)


def _attn_body(q_ref, k_ref, v_ref, tri_ref, o_ref, c_ref, acc_ref, *, ns, hd):
    g, depth = ATTN_BLOCK, ATTN_DEPTH
    qi = pl.program_id(2)
    base = qi * ns
    lane = lax.broadcasted_iota(jnp.int32, (1, 2 * hd), 1)
    head_lanes = (lane < hd, lane >= hd)
    row = lax.broadcasted_iota(jnp.int32, (g, 2 * g), 0)
    col = lax.broadcasted_iota(jnp.int32, (g, 2 * g), 1)
    causal = (col & (g - 1)) < row
    nt = (((1,), (1,)), ((), ()))

    def mask_first(z):
        top = jnp.where(causal, z[:g], MASKED_SCORE)
        return top if z.shape[0] == g else jnp.concatenate([top, z[g:]], axis=0)

    def per_head(x):
        return jnp.concatenate([jnp.where(hl, x, jnp.zeros_like(x)) for hl in head_lanes], axis=0)

    def scores(kb, r_lo, r_hi, masked):
        rows = pl.ds(r_lo * g, (r_hi - r_lo + 1) * g)
        k = k_ref[0, pl.ds(pl.multiple_of(kb * g, g), g), :]
        z = lax.dot_general(q_ref[0, rows, :], per_head(k), nt, preferred_element_type=jnp.float32)
        if masked:
            z = mask_first(z)
        sp = _softplus2(z)
        suffix = _suffix_sum(sp, tri_ref[...])
        totals = [suffix[:, h * g:h * g + 1] + sp[:, h * g:h * g + 1] for h in range(2)]
        return z - sp - suffix, totals

    def apply(kb, r_lo, r_hi, first, w, totals):
        m = (r_hi - r_lo + 1) * g
        lo = g if first else 0
        top, rest = pl.ds(r_lo * g, g), pl.ds(r_lo * g + lo, m - lo)
        v = v_ref[0, pl.ds(pl.multiple_of(kb * g, g), g), :]
        pieces = [jnp.exp2(w[:g])] if first else []
        if m > lo:
            c = [c_ref[h, rest, :] for h in range(2)]
            c2 = jnp.concatenate([jnp.broadcast_to(ch, (m - lo, g)) for ch in c], axis=1)
            pieces.append(jnp.exp2(w[lo:] - c2))
        a = pieces[0] if len(pieces) == 1 else jnp.concatenate(pieces, axis=0)
        pv = _dot(a.astype(jnp.bfloat16), per_head(v))
        if first:
            acc_ref[top, :] = pv[:g]
            for h in range(2):
                c_ref[h, top, :] = totals[h][:g]
        if m > lo:
            acc_ref[rest, :] += pv[lo:]
            for h in range(2):
                c_ref[h, rest, :] = c[h] + totals[h][lo:]

    def step(*blk):
        apply(*blk, *scores(*blk))

    def dead(r_lo):
        return jnp.min(c_ref[:, r_lo * g:, :]) >= DEAD_LOG2

    def band(min_off):
        blks = [(base + off, max(off, 0), min(off + depth - 1, ns - 1), off >= 0)
                for off in range(ns - 1, max(-depth, min_off - 1), -1)]
        partial = [scores(*blk) for blk in blks]
        for blk, (w, totals) in zip(blks, partial):
            apply(*blk, w, totals)

    def ramp(min_off, rows_dead=None):
        offs = [off for off in range(ns - 1 - depth, -depth, -1) if off >= min_off]
        if offs:
            if rows_dead is None:
                rows_dead = dead(offs[-1] + depth)

            @pl.when(jnp.logical_not(rows_dead))
            def _():
                for off in offs:
                    step(base + off, off + depth, ns - 1, False)

    @pl.when(qi == 0)
    def _():
        band(0)
        ramp(0)

    @pl.when(qi > 0)
    def _():
        band(1 - depth)
        tile_dead = dead(0)
        ramp(1 - depth, tile_dead)

        def body(carry):
            kb, _ = carry
            step(kb, 0, ns - 1, False)
            return kb - 1, dead(0).astype(jnp.int32)

        lax.while_loop(lambda carry: (carry[0] >= 0) & (carry[1] == 0), body,
                       (base - depth, tile_dead.astype(jnp.int32)))

    o_ref[0] = acc_ref[...].astype(o_ref.dtype)


def _attn(qkv, *, n_heads, tq=2048):
    b, s, n = qkv.shape
    d_attn = n // 3
    hd = d_attn // n_heads
    g = ATTN_BLOCK
    ns = tq // g
    assert 2 * hd == V7X_LANES and s % tq == 0 and tq % g == 0 and ns >= ATTN_DEPTH - 1
    pairs = n_heads // 2
    tri = jnp.tril(jnp.ones((g, g), jnp.bfloat16), -1)
    tri = jnp.kron(jnp.eye(2, dtype=jnp.bfloat16), tri)
    return pl.pallas_call(
        functools.partial(_attn_body, ns=ns, hd=hd),
        out_shape=jax.ShapeDtypeStruct((b, s, d_attn), jnp.bfloat16),
        grid=(b, pairs, s // tq),
        in_specs=[
            pl.BlockSpec((1, tq, 2 * hd), lambda bi, p, i: (bi, i, p)),
            pl.BlockSpec((1, s, 2 * hd), lambda bi, p, i: (bi, 0, pairs + p)),
            pl.BlockSpec((1, s, 2 * hd), lambda bi, p, i: (bi, 0, 2 * pairs + p)),
            _resident((2 * g, 2 * g)),
        ],
        out_specs=pl.BlockSpec((1, tq, 2 * hd), lambda bi, p, i: (bi, i, p)),
        scratch_shapes=[pltpu.VMEM((2, tq, 1), jnp.float32),
                        pltpu.VMEM((tq, 2 * hd), jnp.float32)],
        compiler_params=_params(("arbitrary", "arbitrary", "arbitrary")),
        name="attn",
    )(qkv, qkv, qkv, tri)


def kernel(x, norm_g, final_g, ffn_w_in, ffn_w_out, conv_w_pw1, conv_b_pw1, conv_w_dw, conv_b_dw,
           conv_ln_g, conv_ln_b, conv_w_pw2, conv_b_pw2, attn_w_qkv, attn_w_o):
    b, s, d = x.shape
    depth = norm_g.shape[0]
    x = x.reshape(b * s, d)
    for i in range(depth):
        g = norm_g[i]
        j = i // 2
        x = _ffn(x, g[0], ffn_w_in, ffn_w_out, (i, 0))
        pre = None
        if i % 2 == 0:
            x = _conv(x.reshape(b, s, d), g[1], conv_w_pw1[j], conv_b_pw1[j], conv_w_dw[j],
                      conv_b_dw[j], conv_ln_g[j], conv_ln_b[j], conv_w_pw2[j],
                      conv_b_pw2[j]).reshape(b * s, d)
        else:
            hd = attn_w_qkv.shape[2] // 3 // N_HEADS
            qkv = _qkv(x, g[1], attn_w_qkv[j], scale=hd ** -0.5 * LOG2_E)
            a = _attn(qkv.reshape(b, s, -1), n_heads=N_HEADS)
            pre = (a.reshape(b * s, -1), attn_w_o[j])
        x = _ffn(x, g[2], ffn_w_in, ffn_w_out, (i, 1), pre=pre,
                 final_g=final_g if i == depth - 1 else None)
    return x.reshape(b, s, d)
```
